```python
import math
import jax, jax.numpy as jnp
from jax import lax
import numpy as np

D_MODEL = 2048
BATCH = 8
SEQ = 4096
DEPTH = 2

CHUNK = 64
Q_BLOCK = 128
N_MIXERS = 2
N_CONV_LAYERS = (DEPTH + 1) // 2
N_ATTN_LAYERS = DEPTH // 2
N_DENSE_FFN_LAYERS = (DEPTH + 1) // 2
N_MOE_LAYERS = DEPTH // 2
CONV_WIDTH = 31
DIFF_HEADS = 16
DIFF_HEAD_DIM = D_MODEL // (2 * DIFF_HEADS)
D_FF_DENSE = ((8 * D_MODEL // 3 + 255) // 256) * 256
N_EXPERTS = 8
TOP_K = 2
D_FF_EXPERT = 7 * D_MODEL // 2
RMS_EPS = 1e-6
LN_EPS = 1e-5

kernel_name = "hybrid_conformer_diffattn_moe_adaln"


def rms_norm(x, g):
    xf = x.astype(jnp.float32)
    y = xf * lax.rsqrt(jnp.mean(xf * xf, axis=-1, keepdims=True) + RMS_EPS)
    return (y * g.astype(jnp.float32)).astype(x.dtype)


def layer_norm(x, g, b):
    xf = x.astype(jnp.float32)
    mu = jnp.mean(xf, axis=-1, keepdims=True)
    var = jnp.mean(jnp.square(xf - mu), axis=-1, keepdims=True)
    y = (xf - mu) * lax.rsqrt(var + LN_EPS)
    return (y * g.astype(jnp.float32) + b.astype(jnp.float32)).astype(x.dtype)


def modulate(h, shift, scale):
    return h * (1.0 + scale[:, None, :]) + shift[:, None, :]


def alibi_slopes(n_heads):
    ratio = 2.0 ** (-8.0 / n_heads)
    return jnp.asarray(np.array([ratio ** (h + 1) for h in range(n_heads)], dtype=np.float32))


def diff_lambda_init(layer_idx):
    return 0.8 - 0.6 * math.exp(-0.3 * layer_idx)


def conformer_conv(h, w_in, b_in, w_dw, b_dw, ln_g, ln_b, w_out, b_out):
    u = h @ w_in + b_in
    val, gate = jnp.split(u, 2, axis=-1)
    u = val * jax.nn.sigmoid(gate)
    u = lax.conv_general_dilated(
        u, w_dw[:, None, :].astype(u.dtype),
        window_strides=(1,), padding=((CONV_WIDTH - 1, 0),),
        dimension_numbers=("NWC", "WIO", "NWC"),
        feature_group_count=D_MODEL) + b_dw
    u = jax.nn.silu(layer_norm(u, ln_g, ln_b))
    return u @ w_out + b_out


def diff_attention(h, w_qkv, w_o, lam_q1, lam_k1, lam_q2, lam_k2, subln_g, lambda_init):
    B, S, _ = h.shape
    H, dh = DIFF_HEADS, DIFF_HEAD_DIM
    q, k, v = jnp.split(h @ w_qkv, 3, axis=-1)
    q = q.reshape(B, S, H, 2, dh).astype(jnp.float32) * (dh ** -0.5)
    k = k.reshape(B, S, H, 2, dh).astype(jnp.float32)
    v = v.reshape(B, S, H, 2 * dh)
    lam = (jnp.exp(jnp.sum(lam_q1.astype(jnp.float32) * lam_k1.astype(jnp.float32)))
           - jnp.exp(jnp.sum(lam_q2.astype(jnp.float32) * lam_k2.astype(jnp.float32)))
           + lambda_init)
    slopes = alibi_slopes(H)
    pos = jnp.arange(S, dtype=jnp.int32)
    chunk_id = pos // CHUNK
    outs = []
    for qb in range(S // Q_BLOCK):
        q0 = qb * Q_BLOCK
        kend = q0 + Q_BLOCK
        s = jnp.einsum("bqhmd,bkhmd->bhmqk", q[:, q0:kend], k[:, :kend])
        dist = jnp.abs(pos[q0:kend, None] - pos[None, :kend]).astype(jnp.float32)
        bias = -slopes[:, None, None] * dist[None]
        allowed = chunk_id[None, :kend] <= chunk_id[q0:kend, None]
        s = jnp.where(allowed, s + bias[None, :, None], -jnp.inf)
        p = jax.nn.softmax(s, axis=-1)
        attn = p[:, :, 0] - lam * p[:, :, 1]
        outs.append(jnp.einsum("bhqk,bkhe->bqhe", attn.astype(v.dtype), v[:, :kend]))
    o = jnp.concatenate(outs, axis=1)
    o = rms_norm(o, subln_g) * (1.0 - lambda_init)
    return o.reshape(B, S, H * 2 * dh) @ w_o


def swiglu(h, w_gate, w_up, w_down):
    return (jax.nn.silu(h @ w_gate) * (h @ w_up)) @ w_down


def moe_swiglu(h, w_router, w_gate, w_up, w_down):
    B, S, D = h.shape
    t = h.reshape(B * S, D)
    logits = (t @ w_router).astype(jnp.float32)
    top_v, top_i = lax.top_k(logits, TOP_K)
    top_w = jax.nn.softmax(top_v, axis=-1)
    combine = jnp.sum(jax.nn.one_hot(top_i, N_EXPERTS, dtype=jnp.float32) * top_w[..., None], axis=1)
    y = jnp.zeros_like(t)
    for e in range(N_EXPERTS):
        y = y + combine[:, e:e + 1].astype(t.dtype) * swiglu(t, w_gate[e], w_up[e], w_down[e])
    return y.reshape(B, S, D)


def setup_inputs(seed: int = 0) -> dict:
    key = jax.random.key(seed)
    ks = list(jax.random.split(key, 32))
    it = iter(ks)
    D = D_MODEL

    def nrm(shape, scale):
        return scale * jax.random.normal(next(it), shape, jnp.float32)

    nA, nB, nD, nE = N_CONV_LAYERS, N_ATTN_LAYERS, N_DENSE_FFN_LAYERS, N_MOE_LAYERS
    return {
        "x": nrm((BATCH, SEQ, D), 1.0),
        "c": nrm((BATCH, D), 1.0),
        "mod_w": nrm((DEPTH, D, 6 * D), 0.3 * D ** -0.5),
        "mod_b": nrm((DEPTH, 6 * D), 0.02),
        "norm1_g": 1.0 + nrm((DEPTH, D), 0.02),
        "norm2_g": 1.0 + nrm((DEPTH, D), 0.02),
        "conv_w_in": nrm((nA, D, 2 * D), D ** -0.5),
        "conv_b_in": nrm((nA, 2 * D), 0.02),
        "conv_w_dw": nrm((nA, CONV_WIDTH, D), CONV_WIDTH ** -0.5),
        "conv_b_dw": nrm((nA, D), 0.02),
        "conv_ln_g": 1.0 + nrm((nA, D), 0.02),
        "conv_ln_b": nrm((nA, D), 0.02),
        "conv_w_out": nrm((nA, D, D), D ** -0.5),
        "conv_b_out": nrm((nA, D), 0.02),
        "attn_w_qkv": nrm((nB, D, 3 * D), D ** -0.5),
        "attn_w_o": nrm((nB, D, D), D ** -0.5),
        "attn_lam_q1": nrm((nB, DIFF_HEAD_DIM), 0.1),
        "attn_lam_k1": nrm((nB, DIFF_HEAD_DIM), 0.1),
        "attn_lam_q2": nrm((nB, DIFF_HEAD_DIM), 0.1),
        "attn_lam_k2": nrm((nB, DIFF_HEAD_DIM), 0.1),
        "attn_subln_g": 1.0 + nrm((nB, 2 * DIFF_HEAD_DIM), 0.02),
        "ffn_w_gate": nrm((nD, D, D_FF_DENSE), D ** -0.5),
        "ffn_w_up": nrm((nD, D, D_FF_DENSE), D ** -0.5),
        "ffn_w_down": nrm((nD, D_FF_DENSE, D), D_FF_DENSE ** -0.5),
        "moe_w_router": nrm((nE, D, N_EXPERTS), D ** -0.5),
        "moe_w_gate": nrm((nE, N_EXPERTS, D, D_FF_EXPERT), D ** -0.5),
        "moe_w_up": nrm((nE, N_EXPERTS, D, D_FF_EXPERT), D ** -0.5),
        "moe_w_down": nrm((nE, N_EXPERTS, D_FF_EXPERT, D), D_FF_EXPERT ** -0.5),
        "final_g": 1.0 + nrm((D,), 0.02),
    }


def reference(x, c, mod_w, mod_b, norm1_g, norm2_g,
              conv_w_in, conv_b_in, conv_w_dw, conv_b_dw, conv_ln_g, conv_ln_b, conv_w_out, conv_b_out,
              attn_w_qkv, attn_w_o, attn_lam_q1, attn_lam_k1, attn_lam_q2, attn_lam_k2, attn_subln_g,
              ffn_w_gate, ffn_w_up, ffn_w_down,
              moe_w_router, moe_w_gate, moe_w_up, moe_w_down,
              final_g):
    c_act = jax.nn.silu(c)
    for i in range(DEPTH):
        j = i // N_MIXERS
        mod = c_act @ mod_w[i] + mod_b[i]
        sh1, sc1, g1, sh2, sc2, g2 = jnp.split(mod, 6, axis=-1)
        h = modulate(rms_norm(x, norm1_g[i]), sh1, sc1)
        if i % N_MIXERS == 0:
            mix = conformer_conv(h, conv_w_in[j], conv_b_in[j], conv_w_dw[j], conv_b_dw[j],
                                 conv_ln_g[j], conv_ln_b[j], conv_w_out[j], conv_b_out[j])
        else:
            mix = diff_attention(h, attn_w_qkv[j], attn_w_o[j], attn_lam_q1[j], attn_lam_k1[j],
                                 attn_lam_q2[j], attn_lam_k2[j], attn_subln_g[j], diff_lambda_init(i))
        x = x + g1[:, None, :] * mix
        h = modulate(rms_norm(x, norm2_g[i]), sh2, sc2)
        if i % 2 == 0:
            ffn = swiglu(h, ffn_w_gate[j], ffn_w_up[j], ffn_w_down[j])
        else:
            ffn = moe_swiglu(h, moe_w_router[j], moe_w_gate[j], moe_w_up[j], moe_w_down[j])
        x = x + g2[:, None, :] * ffn
    return rms_norm(x, final_g)
```

```python
import functools
import math

import numpy as np
import jax
import jax.numpy as jnp
from jax import lax
from jax.experimental import pallas as pl
from jax.experimental.pallas import tpu as pltpu

F32 = jnp.float32
BF16 = jnp.bfloat16

RMS_EPS = 1e-6
LN_EPS = 1e-5
CHUNK = 64
CONV_WIDTH = 31
HEAD_DIM = 64
HEAD_LANES = 2 * HEAD_DIM
N_EXPERTS = 8
LANES = 128
SUBLANES = 8
CONV_LANES = 512
HALO = 32
VMEM_LIMIT = 56 * 1024 * 1024


def _tile(n, pref):
    t = min(n, pref)
    while n % t:
        t //= 2
    return t


def _sigmoid(x):
    return 1.0 / (1.0 + jnp.exp(-x))


def _cparams(sem):
    return pltpu.CompilerParams(dimension_semantics=sem, vmem_limit_bytes=VMEM_LIMIT)


def _mod_kernel(c_ref, w_ref, b_ref, o_ref):
    c = c_ref[...]
    ca = (c * _sigmoid(c)).astype(BF16)
    o_ref[0] = jnp.dot(ca, w_ref[0].astype(BF16), preferred_element_type=F32) + b_ref[0]


def _modulation(c, mod_w, mod_b):
    depth, d, n6 = mod_w.shape
    nb = c.shape[0]
    tn = _tile(n6, 1024)
    return pl.pallas_call(
        _mod_kernel,
        grid=(depth, n6 // tn),
        in_specs=[
            pl.BlockSpec((nb, d), lambda l, j: (0, 0)),
            pl.BlockSpec((1, d, tn), lambda l, j: (l, 0, j)),
            pl.BlockSpec((1, 1, tn), lambda l, j: (l, 0, j)),
        ],
        out_specs=pl.BlockSpec((1, nb, tn), lambda l, j: (l, 0, j)),
        out_shape=jax.ShapeDtypeStruct((depth, nb, n6), F32),
        compiler_params=_cparams(("arbitrary", "arbitrary")),
        name="modulation",
    )(c, mod_w, mod_b.reshape(depth, 1, n6))


def _norm_mod_rows(x_ref, g, shift, scale, h_ref, rows, rc):
    one_plus = 1.0 + scale

    def body(r, carry):
        rs = pl.multiple_of(r * rc, rc)
        xv = x_ref[0, pl.ds(rs, rc), :]
        inv = lax.rsqrt(jnp.mean(xv * xv, axis=-1, keepdims=True) + RMS_EPS)
        h = ((xv * inv) * g) * one_plus + shift
        h_ref[pl.ds(rs, rc), :] = h.astype(h_ref.dtype)
        return carry

    lax.fori_loop(0, rows // rc, body, 0)


def _conv_in_kernel(x_ref, mod_ref, g_ref, wv_ref, wg_ref, bv_ref, bg_ref, o_ref, h_ref, *, rc):
    @pl.when(pl.program_id(2) == 0)
    def _():
        _norm_mod_rows(x_ref, g_ref[...], mod_ref[0, 0:1, :], mod_ref[0, 1:2, :], h_ref,
                       h_ref.shape[0], rc)

    h = h_ref[...]
    val = jnp.dot(h, wv_ref[...], preferred_element_type=F32) + bv_ref[...]
    gate = jnp.dot(h, wg_ref[...], preferred_element_type=F32) + bg_ref[...]
    o_ref[0] = (val * _sigmoid(gate)).astype(o_ref.dtype)


def _conv_in(x, mod, norm_g, w_in, b_in):
    nb, s, d = x.shape
    tm = _tile(s, 1024)
    tn = _tile(d, 512)
    nj = d // tn
    b2 = b_in.reshape(1, 2 * d)
    return pl.pallas_call(
        functools.partial(_conv_in_kernel, rc=_tile(tm, 32)),
        grid=(nb, s // tm, nj),
        in_specs=[
            pl.BlockSpec((1, tm, d), lambda b, i, j: (b, i, 0)),
            pl.BlockSpec((1, 6, d), lambda b, i, j: (b, 0, 0)),
            pl.BlockSpec((1, d), lambda b, i, j: (0, 0)),
            pl.BlockSpec((d, tn), lambda b, i, j: (0, j)),
            pl.BlockSpec((d, tn), lambda b, i, j: (0, j + nj)),
            pl.BlockSpec((1, tn), lambda b, i, j: (0, j)),
            pl.BlockSpec((1, tn), lambda b, i, j: (0, j + nj)),
        ],
        out_specs=pl.BlockSpec((1, tm, tn), lambda b, i, j: (b, i, j)),
        out_shape=jax.ShapeDtypeStruct((nb, s, d), BF16),
        scratch_shapes=[pltpu.VMEM((tm, d), BF16)],
        compiler_params=_cparams(("arbitrary", "arbitrary", "arbitrary")),
        name="conv_in_glu",
    )(x, mod, norm_g.reshape(1, d), w_in, w_in, b2, b2)


def _qkv_kernel(x_ref, mod_ref, g_ref, w_ref, o_ref, h_ref, *, rc):
    @pl.when(pl.program_id(2) == 0)
    def _():
        _norm_mod_rows(x_ref, g_ref[...], mod_ref[0, 0:1, :], mod_ref[0, 1:2, :], h_ref,
                       h_ref.shape[0], rc)

    o_ref[0] = jnp.dot(h_ref[...], w_ref[...], preferred_element_type=F32).astype(o_ref.dtype)


def _qkv_proj(x, mod, norm_g, w_qkv):
    nb, s, d = x.shape
    n3 = w_qkv.shape[1]
    tm = _tile(s, 1024)
    tn = _tile(n3, 512)
    return pl.pallas_call(
        functools.partial(_qkv_kernel, rc=_tile(tm, 32)),
        grid=(nb, s // tm, n3 // tn),
        in_specs=[
            pl.BlockSpec((1, tm, d), lambda b, i, j: (b, i, 0)),
            pl.BlockSpec((1, 6, d), lambda b, i, j: (b, 0, 0)),
            pl.BlockSpec((1, d), lambda b, i, j: (0, 0)),
            pl.BlockSpec((d, tn), lambda b, i, j: (0, j)),
        ],
        out_specs=pl.BlockSpec((1, tm, tn), lambda b, i, j: (b, i, j)),
        out_shape=jax.ShapeDtypeStruct((nb, s, n3), BF16),
        scratch_shapes=[pltpu.VMEM((tm, d), BF16)],
        compiler_params=_cparams(("arbitrary", "arbitrary", "arbitrary")),
        name="qkv_proj",
    )(x, mod, norm_g.reshape(1, d), w_qkv)


def _conv_out_kernel(u_ref, halo_ref, x_ref, mod_ref, wdw_ref, bdw_ref, lng_ref, lnb_ref,
                     wo_ref, bo_ref, o_ref, buf_ref, cv_ref, h_ref, *, rc, cr):
    tm = u_ref.shape[1]
    first = pl.program_id(1) == 0
    halo = halo_ref[0].astype(F32)
    buf_ref[0:HALO, :] = jnp.where(first, jnp.zeros_like(halo), halo)

    def fill(r, carry):
        rs = pl.multiple_of(r * rc, rc)
        buf_ref[pl.ds(HALO + rs, rc), :] = u_ref[0, pl.ds(rs, rc), :].astype(F32)
        return carry

    lax.fori_loop(0, tm // rc, fill, 0)

    lng = lng_ref[...]
    lnb = lnb_ref[...]
    lead = HALO - (CONV_WIDTH - 1)
    d = buf_ref.shape[1]
    cw = _tile(d, CONV_LANES)
    win = cr + HALO

    def taps(r, carry):
        rs = pl.multiple_of(r * cr, cr)
        for lb in range(d // cw):
            cols = slice(lb * cw, (lb + 1) * cw)
            w = buf_ref[pl.ds(rs, win), cols]
            acc = jnp.zeros((cr, cw), F32) + bdw_ref[:, cols]
            for sh in range(SUBLANES):
                ws = w if sh == 0 else pltpu.roll(w, win - sh, axis=0)
                for a in range(win // SUBLANES):
                    k = SUBLANES * a + sh - lead
                    if 0 <= k < CONV_WIDTH:
                        acc = acc + wdw_ref[k:k + 1, cols] * ws[SUBLANES * a:SUBLANES * a + cr]
            cv_ref[pl.ds(rs, cr), cols] = acc
        return carry

    lax.fori_loop(0, tm // cr, taps, 0)

    def conv(r, carry):
        rs = pl.multiple_of(r * rc, rc)
        acc = cv_ref[pl.ds(rs, rc), :]
        mu = jnp.mean(acc, axis=-1, keepdims=True)
        cen = acc - mu
        var = jnp.mean(cen * cen, axis=-1, keepdims=True)
        y = (cen * lax.rsqrt(var + LN_EPS)) * lng + lnb
        h_ref[pl.ds(rs, rc), :] = (y * _sigmoid(y)).astype(h_ref.dtype)
        return carry

    lax.fori_loop(0, tm // rc, conv, 0)

    mix = jnp.dot(h_ref[...], wo_ref[...], preferred_element_type=F32) + bo_ref[...]
    o_ref[0] = x_ref[0] + mod_ref[0, 2:3, :] * mix


def _conv_out(u, x, mod, w_dw, b_dw, ln_g, ln_b, w_out, b_out):
    nb, s, d = x.shape
    tm = _tile(s, 512)
    hb = tm // HALO
    row = lambda a: a.reshape(1, d)
    return pl.pallas_call(
        functools.partial(_conv_out_kernel, rc=_tile(tm, 16), cr=_tile(tm, 32)),
        grid=(nb, s // tm),
        in_specs=[
            pl.BlockSpec((1, tm, d), lambda b, i: (b, i, 0)),
            pl.BlockSpec((1, HALO, d), lambda b, i: (b, jnp.maximum(i * hb - 1, 0), 0)),
            pl.BlockSpec((1, tm, d), lambda b, i: (b, i, 0)),
            pl.BlockSpec((1, 6, d), lambda b, i: (b, 0, 0)),
            pl.BlockSpec((CONV_WIDTH, d), lambda b, i: (0, 0)),
            pl.BlockSpec((1, d), lambda b, i: (0, 0)),
            pl.BlockSpec((1, d), lambda b, i: (0, 0)),
            pl.BlockSpec((1, d), lambda b, i: (0, 0)),
            pl.BlockSpec((d, d), lambda b, i: (0, 0)),
            pl.BlockSpec((1, d), lambda b, i: (0, 0)),
        ],
        out_specs=pl.BlockSpec((1, tm, d), lambda b, i: (b, i, 0)),
        out_shape=jax.ShapeDtypeStruct((nb, s, d), F32),
        scratch_shapes=[pltpu.VMEM((HALO + tm, d), F32), pltpu.VMEM((tm, d), F32),
                        pltpu.VMEM((tm, d), BF16)],
        compiler_params=_cparams(("arbitrary", "arbitrary")),
        name="conv_out",
    )(u, u, x, mod, w_dw, row(b_dw), row(ln_g), row(ln_b), w_out, row(b_out))


def _ffn_kernel(x_ref, mod_ref, g_ref, wg_ref, wu_ref, wd_ref, o_ref, h_ref, *, rc):
    j = pl.program_id(2)

    @pl.when(j == 0)
    def _():
        _norm_mod_rows(x_ref, g_ref[...], mod_ref[0, 3:4, :], mod_ref[0, 4:5, :], h_ref,
                       h_ref.shape[0], rc)

    h = h_ref[...]
    gate = jnp.dot(h, wg_ref[...], preferred_element_type=F32)
    up = jnp.dot(h, wu_ref[...], preferred_element_type=F32)
    a = ((gate * _sigmoid(gate)) * up).astype(BF16)
    part = jnp.dot(a, wd_ref[...], preferred_element_type=F32)

    @pl.when(j == 0)
    def _():
        o_ref[0] = part

    @pl.when(j > 0)
    def _():
        o_ref[0] += part

    @pl.when(j == pl.num_programs(2) - 1)
    def _():
        o_ref[0] = x_ref[0] + mod_ref[0, 5:6, :] * o_ref[0]


def _dense_ffn(x, mod, norm_g, w_gate, w_up, w_down):
    nb, s, d = x.shape
    f = w_gate.shape[1]
    tm = _tile(s, 512)
    tf = _tile(f, 512)
    return pl.pallas_call(
        functools.partial(_ffn_kernel, rc=_tile(tm, 32)),
        grid=(nb, s // tm, f // tf),
        in_specs=[
            pl.BlockSpec((1, tm, d), lambda b, i, j: (b, i, 0)),
            pl.BlockSpec((1, 6, d), lambda b, i, j: (b, 0, 0)),
            pl.BlockSpec((1, d), lambda b, i, j: (0, 0)),
            pl.BlockSpec((d, tf), lambda b, i, j: (0, j)),
            pl.BlockSpec((d, tf), lambda b, i, j: (0, j)),
            pl.BlockSpec((tf, d), lambda b, i, j: (j, 0)),
        ],
        out_specs=pl.BlockSpec((1, tm, d), lambda b, i, j: (b, i, 0)),
        out_shape=jax.ShapeDtypeStruct((nb, s, d), F32),
        scratch_shapes=[pltpu.VMEM((tm, d), BF16)],
        compiler_params=_cparams(("arbitrary", "arbitrary", "arbitrary")),
        name="dense_ffn",
    )(x, mod, norm_g.reshape(1, d), w_gate, w_up, w_down)


def _attn_kernel(slopes_ref, q_ref, k_ref, v_ref, lq1_ref, lk1_ref, lq2_ref, lk2_ref, sg_ref,
                 o_ref, *, tk, lambda_init):
    tq = q_ref.shape[1]
    head = pl.program_id(1)
    qi = pl.program_id(2)
    slope = slopes_ref[head]

    lam = (jnp.exp(jnp.sum(lq1_ref[...] * lk1_ref[...], axis=-1, keepdims=True))
           - jnp.exp(jnp.sum(lq2_ref[...] * lk2_ref[...], axis=-1, keepdims=True))
           + lambda_init)

    q = q_ref[0] * (HEAD_DIM ** -0.5)
    lane = lax.broadcasted_iota(jnp.int32, q.shape, 1)
    q1 = jnp.where(lane < HEAD_DIM, q, jnp.zeros_like(q))
    q2 = jnp.where(lane >= HEAD_DIM, q, jnp.zeros_like(q))

    q0 = qi * tq
    qpos = q0 + lax.broadcasted_iota(jnp.int32, (tq, tk), 0)
    kiota = lax.broadcasted_iota(jnp.int32, (tq, tk), 1)
    qchunk = qpos // CHUNK
    nt = (((1,), (1,)), ((), ()))

    def step(c, carry):
        m1, l1, a1, m2, l2, a2 = carry
        ks = pl.multiple_of(c * tk, tk)
        kc = k_ref[0, pl.ds(ks, tk), :]
        vc = v_ref[0, pl.ds(ks, tk), :]
        kpos = ks + kiota
        bias = -slope * jnp.abs(qpos - kpos).astype(F32)
        allowed = (kpos // CHUNK) <= qchunk

        def one(qm, m, l, a):
            sc = lax.dot_general(qm, kc, nt, preferred_element_type=F32)
            sc = jnp.where(allowed, sc + bias, -jnp.inf)
            m_new = jnp.maximum(m, jnp.max(sc, axis=-1, keepdims=True))
            alpha = jnp.exp(m - m_new)
            p = jnp.exp(sc - m_new)
            l_new = alpha * l + jnp.sum(p, axis=-1, keepdims=True)
            a_new = alpha * a + jnp.dot(p.astype(BF16), vc, preferred_element_type=F32)
            return m_new, l_new, a_new

        m1, l1, a1 = one(q1, m1, l1, a1)
        m2, l2, a2 = one(q2, m2, l2, a2)
        return m1, l1, a1, m2, l2, a2

    neg = jnp.full((tq, 1), -jnp.inf, F32)
    zero1 = jnp.zeros((tq, 1), F32)
    zacc = jnp.zeros((tq, HEAD_LANES), F32)
    nkv = (q0 + tq) // tk
    m1, l1, a1, m2, l2, a2 = lax.fori_loop(0, nkv, step, (neg, zero1, zacc, neg, zero1, zacc))

    o = a1 / l1 - lam * (a2 / l2)
    o = o * lax.rsqrt(jnp.mean(o * o, axis=-1, keepdims=True) + RMS_EPS)
    o_ref[0] = ((o * sg_ref[...]) * (1.0 - lambda_init)).astype(o_ref.dtype)


def _diff_attention(qkv, lam_q1, lam_k1, lam_q2, lam_k2, subln_g, lambda_init):
    nb, s, n3 = qkv.shape
    d = n3 // 3
    heads = d // HEAD_LANES
    tq = _tile(s, 256)
    ratio = 2.0 ** (-8.0 / heads)
    slopes = jnp.asarray(np.array([ratio ** (h + 1) for h in range(heads)], dtype=np.float32))
    vec = lambda a: a.reshape(1, HEAD_DIM)
    small = pl.BlockSpec((1, HEAD_DIM), lambda b, h, i: (0, 0))
    return pl.pallas_call(
        functools.partial(_attn_kernel, tk=tq, lambda_init=lambda_init),
        grid=(nb, heads, s // tq),
        in_specs=[
            pl.BlockSpec(memory_space=pltpu.SMEM),
            pl.BlockSpec((1, tq, HEAD_LANES), lambda b, h, i: (b, i, h)),
            pl.BlockSpec((1, s, HEAD_LANES), lambda b, h, i: (b, 0, heads + h)),
            pl.BlockSpec((1, s, HEAD_LANES), lambda b, h, i: (b, 0, 2 * heads + h)),
            small, small, small, small,
            pl.BlockSpec((1, HEAD_LANES), lambda b, h, i: (0, 0)),
        ],
        out_specs=pl.BlockSpec((1, tq, HEAD_LANES), lambda b, h, i: (b, i, h)),
        out_shape=jax.ShapeDtypeStruct((nb, s, d), BF16),
        compiler_params=_cparams(("arbitrary", "arbitrary", "arbitrary")),
        name="diff_attention",
    )(slopes, qkv, qkv, qkv, vec(lam_q1), vec(lam_k1), vec(lam_q2), vec(lam_k2),
      subln_g.reshape(1, HEAD_LANES))


R_E1, R_E2, R_W1, R_W2, R_RANK1, R_RANK2 = range(6)


def _wo_router_kernel(a_ref, x_ref, mod_ref, g_ref, wo_ref, wr_ref, x_out, h_out, r_out, cnt_out,
                      cnt_ref):
    tm = a_ref.shape[1]

    @pl.when((pl.program_id(0) == 0) & (pl.program_id(1) == 0))
    def _():
        cnt_ref[...] = jnp.zeros_like(cnt_ref)

    mix = jnp.dot(a_ref[0], wo_ref[...], preferred_element_type=F32)
    xn = x_ref[0] + mod_ref[0, 2:3, :] * mix
    x_out[0] = xn
    inv = lax.rsqrt(jnp.mean(xn * xn, axis=-1, keepdims=True) + RMS_EPS)
    h = ((xn * inv) * g_ref[...]) * (1.0 + mod_ref[0, 4:5, :]) + mod_ref[0, 3:4, :]
    h_out[0] = h

    logits = jnp.dot(h, wr_ref[...], preferred_element_type=F32, precision=lax.Precision.HIGHEST)
    lane = lax.broadcasted_iota(jnp.int32, logits.shape, 1).astype(F32)
    logits = jnp.where(lane < N_EXPERTS, logits, -jnp.inf)
    v1 = jnp.max(logits, axis=-1, keepdims=True)
    e1 = jnp.min(jnp.where(logits == v1, lane, float(LANES)), axis=-1, keepdims=True)
    hot1 = lane == e1
    rest = jnp.where(hot1, -jnp.inf, logits)
    v2 = jnp.max(rest, axis=-1, keepdims=True)
    e2 = jnp.min(jnp.where(rest == v2, lane, float(LANES)), axis=-1, keepdims=True)
    hot2 = lane == e2
    ex = jnp.exp(v2 - v1)
    w1 = 1.0 / (1.0 + ex)
    w2 = ex / (1.0 + ex)

    hot = jnp.where(hot1 | hot2, 1.0, 0.0)
    ri = lax.broadcasted_iota(jnp.int32, (tm, tm), 0)
    ci = lax.broadcasted_iota(jnp.int32, (tm, tm), 1)
    tri = jnp.where(ci < ri, 1.0, 0.0).astype(BF16)
    before = jnp.dot(tri, hot.astype(BF16), preferred_element_type=F32) + cnt_ref[...]
    rank1 = jnp.sum(jnp.where(hot1, before, 0.0), axis=-1, keepdims=True)
    rank2 = jnp.sum(jnp.where(hot2, before, 0.0), axis=-1, keepdims=True)
    cnt_ref[...] += jnp.sum(hot, axis=0, keepdims=True)
    cnt_out[...] = cnt_ref[...]

    rec = jnp.where(lane == R_E1, e1, 0.0)
    rec = jnp.where(lane == R_E2, e2, rec)
    rec = jnp.where(lane == R_W1, w1, rec)
    rec = jnp.where(lane == R_W2, w2, rec)
    rec = jnp.where(lane == R_RANK1, rank1, rec)
    rec = jnp.where(lane == R_RANK2, rank2, rec)
    r_out[0] = rec


def _wo_router(attn, x, mod, norm_g, w_o, w_router):
    nb, s, d = x.shape
    tm = _tile(s, 256)
    wr = jnp.zeros((d, LANES), F32).at[:, :N_EXPERTS].set(w_router)
    return pl.pallas_call(
        _wo_router_kernel,
        grid=(nb, s // tm),
        in_specs=[
            pl.BlockSpec((1, tm, d), lambda b, i: (b, i, 0)),
            pl.BlockSpec((1, tm, d), lambda b, i: (b, i, 0)),
            pl.BlockSpec((1, 6, d), lambda b, i: (b, 0, 0)),
            pl.BlockSpec((1, d), lambda b, i: (0, 0)),
            pl.BlockSpec((d, d), lambda b, i: (0, 0)),
            pl.BlockSpec((d, LANES), lambda b, i: (0, 0)),
        ],
        out_specs=[
            pl.BlockSpec((1, tm, d), lambda b, i: (b, i, 0)),
            pl.BlockSpec((1, tm, d), lambda b, i: (b, i, 0)),
            pl.BlockSpec((1, tm, LANES), lambda b, i: (b, i, 0)),
            pl.BlockSpec((1, LANES), lambda b, i: (0, 0)),
        ],
        out_shape=[
            jax.ShapeDtypeStruct((nb, s, d), F32),
            jax.ShapeDtypeStruct((nb, s, d), F32),
            jax.ShapeDtypeStruct((nb, s, LANES), F32),
            jax.ShapeDtypeStruct((1, LANES), F32),
        ],
        scratch_shapes=[pltpu.VMEM((1, LANES), F32)],
        compiler_params=_cparams(("arbitrary", "arbitrary")),
        name="wo_router",
    )(attn, x, mod, norm_g.reshape(1, d), w_o, wr)


def _dispatch_kernel(pos_ref, h_hbm, xs_in, xs_out, sem, *, td):
    del xs_in
    base = pl.program_id(0) * td

    def row_copy(t, p):
        return pltpu.make_async_copy(h_hbm.at[pl.ds(base + t, 1)], xs_out.at[pl.ds(p, 1)], sem)

    def issue(t, carry):
        row_copy(t, pos_ref[2 * t]).start()
        row_copy(t, pos_ref[2 * t + 1]).start()
        return carry

    lax.fori_loop(0, td, issue, 0)

    def drain(t, carry):
        row_copy(0, 0).wait()
        row_copy(0, 0).wait()
        return carry

    lax.fori_loop(0, td, drain, 0)


def _dispatch(h2, pos, xs_zero):
    n, d = h2.shape
    td = _tile(n, 1024)
    return pl.pallas_call(
        functools.partial(_dispatch_kernel, td=td),
        grid=(n // td,),
        in_specs=[
            pl.BlockSpec((2 * td,), lambda i: (i,), memory_space=pltpu.SMEM),
            pl.BlockSpec(memory_space=pl.ANY),
            pl.BlockSpec(memory_space=pl.ANY),
        ],
        out_specs=pl.BlockSpec(memory_space=pl.ANY),
        out_shape=jax.ShapeDtypeStruct(xs_zero.shape, xs_zero.dtype),
        scratch_shapes=[pltpu.SemaphoreType.DMA(())],
        input_output_aliases={2: 0},
        compiler_params=_cparams(("arbitrary",)),
        name="dispatch",
    )(pos.reshape(-1), h2, xs_zero)


def _moe_kernel(be_ref, bv_ref, xs_ref, wg_ref, wu_ref, wd_ref, o_ref, xb_ref):
    i = pl.program_id(0)
    j = pl.program_id(1)
    valid = bv_ref[i] > 0

    @pl.when(valid & (j == 0))
    def _():
        xb_ref[...] = xs_ref[...].astype(BF16)

    @pl.when(valid)
    def _():
        xb = xb_ref[...]
        gate = jnp.dot(xb, wg_ref[0], preferred_element_type=F32)
        up = jnp.dot(xb, wu_ref[0], preferred_element_type=F32)
        a = ((gate * _sigmoid(gate)) * up).astype(BF16)
        part = jnp.dot(a, wd_ref[0], preferred_element_type=F32)

        @pl.when(j == 0)
        def _():
            o_ref[...] = part

        @pl.when(j > 0)
        def _():
            o_ref[...] += part

    @pl.when(jnp.logical_not(valid) & (j == 0))
    def _():
        o_ref[...] = jnp.zeros_like(o_ref)


def _moe_experts(xs, blk_expert, blk_valid, w_gate, w_up, w_down, tme):
    p, d = xs.shape
    f = w_gate.shape[2]
    tf = _tile(f, 512)
    nf = f // tf

    def fj(i, j, bv):
        return jnp.where(bv[i] > 0, j, nf - 1)

    grid_spec = pltpu.PrefetchScalarGridSpec(
        num_scalar_prefetch=2,
        grid=(p // tme, nf),
        in_specs=[
            pl.BlockSpec((tme, d), lambda i, j, be, bv: (i, 0)),
            pl.BlockSpec((1, d, tf), lambda i, j, be, bv: (be[i], 0, fj(i, j, bv))),
            pl.BlockSpec((1, d, tf), lambda i, j, be, bv: (be[i], 0, fj(i, j, bv))),
            pl.BlockSpec((1, tf, d), lambda i, j, be, bv: (be[i], fj(i, j, bv), 0)),
        ],
        out_specs=pl.BlockSpec((tme, d), lambda i, j, be, bv: (i, 0)),
        scratch_shapes=[pltpu.VMEM((tme, d), BF16)],
    )
    return pl.pallas_call(
        _moe_kernel,
        grid_spec=grid_spec,
        out_shape=jax.ShapeDtypeStruct((p, d), F32),
        compiler_params=_cparams(("arbitrary", "arbitrary")),
        name="moe_experts",
    )(blk_expert, blk_valid, xs, w_gate, w_up, w_down)


def _final_kernel(pos_ref, ys_hbm, x_ref, r_ref, mod_ref, g_ref, o_ref, buf_ref, sem):
    tm = x_ref.shape[1]

    def row_copy(t, k, p):
        return pltpu.make_async_copy(ys_hbm.at[pl.ds(p, 1)], buf_ref.at[k, pl.ds(t, 1)], sem)

    def issue(t, carry):
        row_copy(t, 0, pos_ref[2 * t]).start()
        row_copy(t, 1, pos_ref[2 * t + 1]).start()
        return carry

    lax.fori_loop(0, tm, issue, 0)

    def drain(t, carry):
        row_copy(0, 0, 0).wait()
        row_copy(0, 1, 0).wait()
        return carry

    lax.fori_loop(0, tm, drain, 0)

    rec = r_ref[0]
    w1 = rec[:, R_W1:R_W1 + 1]
    w2 = rec[:, R_W2:R_W2 + 1]
    y = w1 * buf_ref[0] + w2 * buf_ref[1]
    xn = x_ref[0] + mod_ref[0, 5:6, :] * y
    inv = lax.rsqrt(jnp.mean(xn * xn, axis=-1, keepdims=True) + RMS_EPS)
    o_ref[0] = (xn * inv) * g_ref[...]


def _combine_final(ys, pos, x, route, mod, final_g):
    nb, s, d = x.shape
    tm = _tile(s, 512)
    ni = s // tm
    return pl.pallas_call(
        _final_kernel,
        grid=(nb, ni),
        in_specs=[
            pl.BlockSpec((2 * tm,), lambda b, i: (b * ni + i,), memory_space=pltpu.SMEM),
            pl.BlockSpec(memory_space=pl.ANY),
            pl.BlockSpec((1, tm, d), lambda b, i: (b, i, 0)),
            pl.BlockSpec((1, tm, LANES), lambda b, i: (b, i, 0)),
            pl.BlockSpec((1, 6, d), lambda b, i: (b, 0, 0)),
            pl.BlockSpec((1, d), lambda b, i: (0, 0)),
        ],
        out_specs=pl.BlockSpec((1, tm, d), lambda b, i: (b, i, 0)),
        out_shape=jax.ShapeDtypeStruct((nb, s, d), F32),
        scratch_shapes=[pltpu.VMEM((2, tm, d), F32), pltpu.SemaphoreType.DMA(())],
        compiler_params=_cparams(("arbitrary", "arbitrary")),
        name="combine_final",
    )(pos.reshape(-1), ys, x, route, mod, final_g.reshape(1, d))


def _lambda_init(layer_idx):
    return 0.8 - 0.6 * math.exp(-0.3 * layer_idx)


def kernel(x, c, mod_w, mod_b, norm1_g, norm2_g, conv_w_in, conv_b_in, conv_w_dw, conv_b_dw, conv_ln_g, conv_ln_b, conv_w_out, conv_b_out, attn_w_qkv, attn_w_o, attn_lam_q1, attn_lam_k1, attn_lam_q2, attn_lam_k2, attn_subln_g, ffn_w_gate, ffn_w_up, ffn_w_down, moe_w_router, moe_w_gate, moe_w_up, moe_w_down, final_g):
    nb, s, d = x.shape
    n = nb * s
    bf = lambda w: w.astype(BF16)

    mod = _modulation(c, mod_w, mod_b).reshape(mod_w.shape[0], nb, 6, d)

    u = _conv_in(x, mod[0], norm1_g[0], bf(conv_w_in[0]), conv_b_in[0])
    x = _conv_out(u, x, mod[0], conv_w_dw[0], conv_b_dw[0], conv_ln_g[0], conv_ln_b[0],
                  bf(conv_w_out[0]), conv_b_out[0])
    x = _dense_ffn(x, mod[0], norm2_g[0], bf(ffn_w_gate[0]), bf(ffn_w_up[0]), bf(ffn_w_down[0]))

    qkv = _qkv_proj(x, mod[1], norm1_g[1], bf(attn_w_qkv[0]))
    attn = _diff_attention(qkv, attn_lam_q1[0], attn_lam_k1[0], attn_lam_q2[0], attn_lam_k2[0],
                           attn_subln_g[0], _lambda_init(1))
    x, h2, route, counts = _wo_router(attn, x, mod[1], norm2_g[1], bf(attn_w_o[0]),
                                      moe_w_router[0])

    tme = _tile(n, 512)
    nblk = 2 * n // tme + N_EXPERTS
    route2 = route.reshape(n, LANES)
    eidx = route2[:, R_E1:R_E2 + 1].astype(jnp.int32)
    rank = route2[:, R_RANK1:R_RANK2 + 1].astype(jnp.int32)
    cnt = counts[0, :N_EXPERTS].astype(jnp.int32)
    blocks = (cnt + tme - 1) // tme
    blk_end = jnp.cumsum(blocks)
    start = (blk_end - blocks) * tme
    pos = start[eidx] + rank
    total = blk_end[-1]
    bid = jnp.arange(nblk, dtype=jnp.int32)
    blk_valid = (bid < total).astype(jnp.int32)
    blk_expert = jnp.searchsorted(blk_end, jnp.minimum(bid, total - 1), side="right")
    blk_expert = jnp.minimum(blk_expert, N_EXPERTS - 1).astype(jnp.int32)

    xs = _dispatch(h2.reshape(n, d), pos, jnp.zeros((nblk * tme, d), F32))
    ys = _moe_experts(xs, blk_expert, blk_valid, bf(moe_w_gate[0]), bf(moe_w_up[0]),
                      bf(moe_w_down[0]), tme)
    return _combine_final(ys, pos, x, route, mod[1], final_g)
```

```python
import functools
import math

import numpy as np
import jax
import jax.numpy as jnp
from jax import lax
from jax.experimental import pallas as pl
from jax.experimental.pallas import tpu as pltpu

F32 = jnp.float32
BF16 = jnp.bfloat16

RMS_EPS = 1e-6
LN_EPS = 1e-5
CHUNK = 64
CONV_WIDTH = 31
HEAD_DIM = 64
HEAD_LANES = 2 * HEAD_DIM
LOG2E = math.log2(math.e)
QUERY_SCALE = HEAD_DIM ** -0.5 * LOG2E
ATTN_BLOCK = 512
N_EXPERTS = 8
LANES = 128
SUBLANES = 8
CONV_LANES = 512
HALO = 32
VMEM_LIMIT = 56 * 1024 * 1024


def _tile(n, pref):
    t = min(n, pref)
    while n % t:
        t //= 2
    return t


def _sigmoid(x):
    return 1.0 / (1.0 + jnp.exp(-x))


def _cparams(sem):
    return pltpu.CompilerParams(dimension_semantics=sem, vmem_limit_bytes=VMEM_LIMIT)


def _mod_kernel(c_ref, w_ref, b_ref, o_ref):
    c = c_ref[...]
    ca = (c * _sigmoid(c)).astype(BF16)
    o_ref[0] = jnp.dot(ca, w_ref[0].astype(BF16), preferred_element_type=F32) + b_ref[0]


def _modulation(c, mod_w, mod_b):
    depth, d, n6 = mod_w.shape
    nb = c.shape[0]
    tn = _tile(n6, 1024)
    return pl.pallas_call(
        _mod_kernel,
        grid=(depth, n6 // tn),
        in_specs=[
            pl.BlockSpec((nb, d), lambda l, j: (0, 0)),
            pl.BlockSpec((1, d, tn), lambda l, j: (l, 0, j)),
            pl.BlockSpec((1, 1, tn), lambda l, j: (l, 0, j)),
        ],
        out_specs=pl.BlockSpec((1, nb, tn), lambda l, j: (l, 0, j)),
        out_shape=jax.ShapeDtypeStruct((depth, nb, n6), F32),
        compiler_params=_cparams(("arbitrary", "arbitrary")),
        name="modulation",
    )(c, mod_w, mod_b.reshape(depth, 1, n6))


def _norm_mod_rows(x_ref, g, shift, scale, h_ref, rows, rc):
    one_plus = 1.0 + scale

    def body(r, carry):
        rs = pl.multiple_of(r * rc, rc)
        xv = x_ref[0, pl.ds(rs, rc), :]
        inv = lax.rsqrt(jnp.mean(xv * xv, axis=-1, keepdims=True) + RMS_EPS)
        h = ((xv * inv) * g) * one_plus + shift
        h_ref[pl.ds(rs, rc), :] = h.astype(h_ref.dtype)
        return carry

    lax.fori_loop(0, rows // rc, body, 0)


def _conv_in_kernel(x_ref, mod_ref, g_ref, wv_ref, wg_ref, bv_ref, bg_ref, o_ref, h_ref, *, rc):
    @pl.when(pl.program_id(2) == 0)
    def _():
        _norm_mod_rows(x_ref, g_ref[...], mod_ref[0, 0:1, :], mod_ref[0, 1:2, :], h_ref,
                       h_ref.shape[0], rc)

    h = h_ref[...]
    val = jnp.dot(h, wv_ref[...], preferred_element_type=F32) + bv_ref[...]
    gate = jnp.dot(h, wg_ref[...], preferred_element_type=F32) + bg_ref[...]
    o_ref[0] = (val * _sigmoid(gate)).astype(o_ref.dtype)


def _conv_in(x, mod, norm_g, w_in, b_in):
    nb, s, d = x.shape
    tm = _tile(s, 1024)
    tn = _tile(d, 512)
    nj = d // tn
    b2 = b_in.reshape(1, 2 * d)
    return pl.pallas_call(
        functools.partial(_conv_in_kernel, rc=_tile(tm, 32)),
        grid=(nb, s // tm, nj),
        in_specs=[
            pl.BlockSpec((1, tm, d), lambda b, i, j: (b, i, 0)),
            pl.BlockSpec((1, 6, d), lambda b, i, j: (b, 0, 0)),
            pl.BlockSpec((1, d), lambda b, i, j: (0, 0)),
            pl.BlockSpec((d, tn), lambda b, i, j: (0, j)),
            pl.BlockSpec((d, tn), lambda b, i, j: (0, j + nj)),
            pl.BlockSpec((1, tn), lambda b, i, j: (0, j)),
            pl.BlockSpec((1, tn), lambda b, i, j: (0, j + nj)),
        ],
        out_specs=pl.BlockSpec((1, tm, tn), lambda b, i, j: (b, i, j)),
        out_shape=jax.ShapeDtypeStruct((nb, s, d), BF16),
        scratch_shapes=[pltpu.VMEM((tm, d), BF16)],
        compiler_params=_cparams(("arbitrary", "arbitrary", "arbitrary")),
        name="conv_in_glu",
    )(x, mod, norm_g.reshape(1, d), w_in, w_in, b2, b2)


def _qkv_kernel(x_ref, mod_ref, g_ref, w_ref, o_ref, h_ref, *, rc, nq):
    @pl.when(pl.program_id(2) == 0)
    def _():
        _norm_mod_rows(x_ref, g_ref[...], mod_ref[0, 0:1, :], mod_ref[0, 1:2, :], h_ref,
                       h_ref.shape[0], rc)

    acc = jnp.dot(h_ref[...], w_ref[...], preferred_element_type=F32)
    is_query = pl.program_id(2) < nq

    @pl.when(is_query)
    def _():
        o_ref[0] = (acc * QUERY_SCALE).astype(o_ref.dtype)

    @pl.when(jnp.logical_not(is_query))
    def _():
        o_ref[0] = acc.astype(o_ref.dtype)


def _qkv_proj(x, mod, norm_g, w_qkv):
    nb, s, d = x.shape
    n3 = w_qkv.shape[1]
    tm = _tile(s, 1024)
    tn = _tile(n3 // 3, 512)
    return pl.pallas_call(
        functools.partial(_qkv_kernel, rc=_tile(tm, 32), nq=n3 // 3 // tn),
        grid=(nb, s // tm, n3 // tn),
        in_specs=[
            pl.BlockSpec((1, tm, d), lambda b, i, j: (b, i, 0)),
            pl.BlockSpec((1, 6, d), lambda b, i, j: (b, 0, 0)),
            pl.BlockSpec((1, d), lambda b, i, j: (0, 0)),
            pl.BlockSpec((d, tn), lambda b, i, j: (0, j)),
        ],
        out_specs=pl.BlockSpec((1, tm, tn), lambda b, i, j: (b, i, j)),
        out_shape=jax.ShapeDtypeStruct((nb, s, n3), BF16),
        scratch_shapes=[pltpu.VMEM((tm, d), BF16)],
        compiler_params=_cparams(("arbitrary", "arbitrary", "arbitrary")),
        name="qkv_proj",
    )(x, mod, norm_g.reshape(1, d), w_qkv)


def _conv_out_kernel(u_ref, halo_ref, x_ref, mod_ref, wdw_ref, bdw_ref, lng_ref, lnb_ref,
                     wo_ref, bo_ref, o_ref, buf_ref, cv_ref, h_ref, *, rc, cr):
    tm = u_ref.shape[1]
    first = pl.program_id(1) == 0
    halo = halo_ref[0].astype(F32)
    buf_ref[0:HALO, :] = jnp.where(first, jnp.zeros_like(halo), halo)

    def fill(r, carry):
        rs = pl.multiple_of(r * rc, rc)
        buf_ref[pl.ds(HALO + rs, rc), :] = u_ref[0, pl.ds(rs, rc), :].astype(F32)
        return carry

    lax.fori_loop(0, tm // rc, fill, 0)

    lng = lng_ref[...]
    lnb = lnb_ref[...]
    lead = HALO - (CONV_WIDTH - 1)
    d = buf_ref.shape[1]
    cw = _tile(d, CONV_LANES)
    win = cr + HALO

    def taps(r, carry):
        rs = pl.multiple_of(r * cr, cr)
        for lb in range(d // cw):
            cols = slice(lb * cw, (lb + 1) * cw)
            w = buf_ref[pl.ds(rs, win), cols]
            acc = jnp.zeros((cr, cw), F32) + bdw_ref[:, cols]
            for sh in range(SUBLANES):
                ws = w if sh == 0 else pltpu.roll(w, win - sh, axis=0)
                for a in range(win // SUBLANES):
                    k = SUBLANES * a + sh - lead
                    if 0 <= k < CONV_WIDTH:
                        acc = acc + wdw_ref[k:k + 1, cols] * ws[SUBLANES * a:SUBLANES * a + cr]
            cv_ref[pl.ds(rs, cr), cols] = acc
        return carry

    lax.fori_loop(0, tm // cr, taps, 0)

    def conv(r, carry):
        rs = pl.multiple_of(r * rc, rc)
        acc = cv_ref[pl.ds(rs, rc), :]
        mu = jnp.mean(acc, axis=-1, keepdims=True)
        cen = acc - mu
        var = jnp.mean(cen * cen, axis=-1, keepdims=True)
        y = (cen * lax.rsqrt(var + LN_EPS)) * lng + lnb
        h_ref[pl.ds(rs, rc), :] = (y * _sigmoid(y)).astype(h_ref.dtype)
        return carry

    lax.fori_loop(0, tm // rc, conv, 0)

    mix = jnp.dot(h_ref[...], wo_ref[...], preferred_element_type=F32) + bo_ref[...]
    o_ref[0] = x_ref[0] + mod_ref[0, 2:3, :] * mix


def _conv_out(u, x, mod, w_dw, b_dw, ln_g, ln_b, w_out, b_out):
    nb, s, d = x.shape
    tm = _tile(s, 512)
    hb = tm // HALO
    row = lambda a: a.reshape(1, d)
    return pl.pallas_call(
        functools.partial(_conv_out_kernel, rc=_tile(tm, 16), cr=_tile(tm, 32)),
        grid=(nb, s // tm),
        in_specs=[
            pl.BlockSpec((1, tm, d), lambda b, i: (b, i, 0)),
            pl.BlockSpec((1, HALO, d), lambda b, i: (b, jnp.maximum(i * hb - 1, 0), 0)),
            pl.BlockSpec((1, tm, d), lambda b, i: (b, i, 0)),
            pl.BlockSpec((1, 6, d), lambda b, i: (b, 0, 0)),
            pl.BlockSpec((CONV_WIDTH, d), lambda b, i: (0, 0)),
            pl.BlockSpec((1, d), lambda b, i: (0, 0)),
            pl.BlockSpec((1, d), lambda b, i: (0, 0)),
            pl.BlockSpec((1, d), lambda b, i: (0, 0)),
            pl.BlockSpec((d, d), lambda b, i: (0, 0)),
            pl.BlockSpec((1, d), lambda b, i: (0, 0)),
        ],
        out_specs=pl.BlockSpec((1, tm, d), lambda b, i: (b, i, 0)),
        out_shape=jax.ShapeDtypeStruct((nb, s, d), F32),
        scratch_shapes=[pltpu.VMEM((HALO + tm, d), F32), pltpu.VMEM((tm, d), F32),
                        pltpu.VMEM((tm, d), BF16)],
        compiler_params=_cparams(("arbitrary", "arbitrary")),
        name="conv_out",
    )(u, u, x, mod, w_dw, row(b_dw), row(ln_g), row(ln_b), w_out, row(b_out))


def _ffn_kernel(x_ref, mod_ref, g_ref, wg_ref, wu_ref, wd_ref, o_ref, h_ref, *, rc):
    j = pl.program_id(2)

    @pl.when(j == 0)
    def _():
        _norm_mod_rows(x_ref, g_ref[...], mod_ref[0, 3:4, :], mod_ref[0, 4:5, :], h_ref,
                       h_ref.shape[0], rc)

    h = h_ref[...]
    gate = jnp.dot(h, wg_ref[...], preferred_element_type=F32)
    up = jnp.dot(h, wu_ref[...], preferred_element_type=F32)
    a = ((gate * _sigmoid(gate)) * up).astype(BF16)
    part = jnp.dot(a, wd_ref[...], preferred_element_type=F32)

    @pl.when(j == 0)
    def _():
        o_ref[0] = part

    @pl.when(j > 0)
    def _():
        o_ref[0] += part

    @pl.when(j == pl.num_programs(2) - 1)
    def _():
        o_ref[0] = x_ref[0] + mod_ref[0, 5:6, :] * o_ref[0]


def _dense_ffn(x, mod, norm_g, w_gate, w_up, w_down):
    nb, s, d = x.shape
    f = w_gate.shape[1]
    tm = _tile(s, 512)
    tf = _tile(f, 512)
    return pl.pallas_call(
        functools.partial(_ffn_kernel, rc=_tile(tm, 32)),
        grid=(nb, s // tm, f // tf),
        in_specs=[
            pl.BlockSpec((1, tm, d), lambda b, i, j: (b, i, 0)),
            pl.BlockSpec((1, 6, d), lambda b, i, j: (b, 0, 0)),
            pl.BlockSpec((1, d), lambda b, i, j: (0, 0)),
            pl.BlockSpec((d, tf), lambda b, i, j: (0, j)),
            pl.BlockSpec((d, tf), lambda b, i, j: (0, j)),
            pl.BlockSpec((tf, d), lambda b, i, j: (j, 0)),
        ],
        out_specs=pl.BlockSpec((1, tm, d), lambda b, i, j: (b, i, 0)),
        out_shape=jax.ShapeDtypeStruct((nb, s, d), F32),
        scratch_shapes=[pltpu.VMEM((tm, d), BF16)],
        compiler_params=_cparams(("arbitrary", "arbitrary", "arbitrary")),
        name="dense_ffn",
    )(x, mod, norm_g.reshape(1, d), w_gate, w_up, w_down)


def _attn_kernel(slopes_ref, q_ref, k_ref, v_ref, dg_ref, lq1_ref, lk1_ref, lq2_ref, lk2_ref,
                 sg_ref, o_ref, vt_ref, *, lambda_init):
    tq = q_ref.shape[1]
    s_len = k_ref.shape[1]
    head = pl.program_id(1)
    qi = pl.program_id(2)
    slope = jnp.full((1, LANES), slopes_ref[head], F32)

    @pl.when(qi == 0)
    def _():
        for c in range(s_len // tq):
            blk = v_ref[0, c * tq:(c + 1) * tq, :].astype(F32)
            vt_ref[:, c * tq:(c + 1) * tq] = blk.T.astype(BF16)

    lam = (jnp.exp(jnp.sum(lq1_ref[...] * lk1_ref[...], axis=-1, keepdims=True))
           - jnp.exp(jnp.sum(lq2_ref[...] * lk2_ref[...], axis=-1, keepdims=True))
           + lambda_init)

    q = q_ref[0]
    lane = lax.broadcasted_iota(jnp.int32, q.shape, 1)
    q1 = jnp.where(lane < HEAD_DIM, q, jnp.zeros_like(q))
    q2 = jnp.where(lane >= HEAD_DIM, q, jnp.zeros_like(q))
    qq = jnp.concatenate([q1, q2], axis=0)
    nt = (((1,), (1,)), ((), ()))

    q0 = qi * tq
    key_off = lax.broadcasted_iota(jnp.int32, (tq, LANES), 0).astype(F32)
    past_bias = jnp.concatenate([slope * key_off] * (2 * tq // LANES), axis=1)

    def update(kc, vtc, bias, shift, state):
        m, l, acc = state
        t = lax.dot_general(kc, qq, nt, preferred_element_type=F32) + bias
        m_new = jnp.maximum(m, jnp.max(t, axis=0, keepdims=True) + shift)
        p = jnp.exp2(t - (m_new - shift))
        alpha = jnp.exp2(m - m_new)
        l_new = alpha * l + jnp.sum(p, axis=0, keepdims=True)
        acc_new = alpha * acc + jnp.dot(vtc, p.astype(BF16), preferred_element_type=F32)
        return m_new, l_new, acc_new

    def past(c, state):
        ks = pl.multiple_of(c * tq, tq)
        shift = slope[:, 0:1] * (ks - q0).astype(F32)
        return update(k_ref[0, pl.ds(ks, tq), :], vt_ref[:, pl.ds(ks, tq)], past_bias, shift, state)

    init = (jnp.full((1, 2 * tq), -jnp.inf, F32), jnp.zeros((1, 2 * tq), F32),
            jnp.zeros((HEAD_LANES, 2 * tq), F32))
    state = lax.fori_loop(0, qi, past, init)
    ks = pl.multiple_of(q0, tq)
    diag = slope[:, 0:1] * dg_ref[...]
    m, l, acc = update(k_ref[0, pl.ds(ks, tq), :], vt_ref[:, pl.ds(ks, tq)],
                       jnp.concatenate([diag, diag], axis=1), jnp.zeros((1, 1), F32), state)

    o = acc[:, :tq] / l[:, :tq] - lam * (acc[:, tq:] / l[:, tq:])
    o = o * lax.rsqrt(jnp.mean(o * o, axis=0, keepdims=True) + RMS_EPS)
    o_ref[0] = ((o.T * sg_ref[...]) * (1.0 - lambda_init)).astype(o_ref.dtype)


def _diag_bias(tq):
    j = np.arange(tq)[:, None]
    r = np.arange(tq)[None, :]
    tile = np.minimum(j, 2 * r - j).astype(np.float32)
    return jnp.asarray(np.where(j // CHUNK <= r // CHUNK, tile, -np.inf).astype(np.float32))


def _diff_attention(qkv, lam_q1, lam_k1, lam_q2, lam_k2, subln_g, lambda_init):
    nb, s, n3 = qkv.shape
    d = n3 // 3
    heads = d // HEAD_LANES
    tq = _tile(s, ATTN_BLOCK)
    ratio = 2.0 ** (-8.0 / heads)
    slopes = np.array([ratio ** (h + 1) for h in range(heads)], dtype=np.float32)
    slopes2 = jnp.asarray(slopes * np.float32(LOG2E))
    vec = lambda a: a.reshape(1, HEAD_DIM)
    small = pl.BlockSpec((1, HEAD_DIM), lambda b, h, i: (0, 0))
    return pl.pallas_call(
        functools.partial(_attn_kernel, lambda_init=lambda_init),
        grid=(nb, heads, s // tq),
        in_specs=[
            pl.BlockSpec(memory_space=pltpu.SMEM),
            pl.BlockSpec((1, tq, HEAD_LANES), lambda b, h, i: (b, i, h)),
            pl.BlockSpec((1, s, HEAD_LANES), lambda b, h, i: (b, 0, heads + h)),
            pl.BlockSpec((1, s, HEAD_LANES), lambda b, h, i: (b, 0, 2 * heads + h)),
            pl.BlockSpec((tq, tq), lambda b, h, i: (0, 0)),
            small, small, small, small,
            pl.BlockSpec((1, HEAD_LANES), lambda b, h, i: (0, 0)),
        ],
        out_specs=pl.BlockSpec((1, tq, HEAD_LANES), lambda b, h, i: (b, i, h)),
        out_shape=jax.ShapeDtypeStruct((nb, s, d), BF16),
        scratch_shapes=[pltpu.VMEM((HEAD_LANES, s), BF16)],
        compiler_params=_cparams(("arbitrary", "arbitrary", "arbitrary")),
        name="diff_attention",
    )(slopes2, qkv, qkv, qkv, _diag_bias(tq), vec(lam_q1), vec(lam_k1), vec(lam_q2), vec(lam_k2),
      subln_g.reshape(1, HEAD_LANES))


R_E1, R_E2, R_W1, R_W2, R_RANK1, R_RANK2 = range(6)


def _wo_router_kernel(a_ref, x_ref, mod_ref, g_ref, wo_ref, wr_ref, x_out, h_out, r_out, cnt_out,
                      cnt_ref):
    tm = a_ref.shape[1]

    @pl.when((pl.program_id(0) == 0) & (pl.program_id(1) == 0))
    def _():
        cnt_ref[...] = jnp.zeros_like(cnt_ref)

    mix = jnp.dot(a_ref[0], wo_ref[...], preferred_element_type=F32)
    xn = x_ref[0] + mod_ref[0, 2:3, :] * mix
    x_out[0] = xn
    inv = lax.rsqrt(jnp.mean(xn * xn, axis=-1, keepdims=True) + RMS_EPS)
    h = ((xn * inv) * g_ref[...]) * (1.0 + mod_ref[0, 4:5, :]) + mod_ref[0, 3:4, :]
    h_out[0] = h

    logits = jnp.dot(h, wr_ref[...], preferred_element_type=F32, precision=lax.Precision.HIGHEST)
    lane = lax.broadcasted_iota(jnp.int32, logits.shape, 1).astype(F32)
    logits = jnp.where(lane < N_EXPERTS, logits, -jnp.inf)
    v1 = jnp.max(logits, axis=-1, keepdims=True)
    e1 = jnp.min(jnp.where(logits == v1, lane, float(LANES)), axis=-1, keepdims=True)
    hot1 = lane == e1
    rest = jnp.where(hot1, -jnp.inf, logits)
    v2 = jnp.max(rest, axis=-1, keepdims=True)
    e2 = jnp.min(jnp.where(rest == v2, lane, float(LANES)), axis=-1, keepdims=True)
    hot2 = lane == e2
    ex = jnp.exp(v2 - v1)
    w1 = 1.0 / (1.0 + ex)
    w2 = ex / (1.0 + ex)

    hot = jnp.where(hot1 | hot2, 1.0, 0.0)
    ri = lax.broadcasted_iota(jnp.int32, (tm, tm), 0)
    ci = lax.broadcasted_iota(jnp.int32, (tm, tm), 1)
    tri = jnp.where(ci < ri, 1.0, 0.0).astype(BF16)
    before = jnp.dot(tri, hot.astype(BF16), preferred_element_type=F32) + cnt_ref[...]
    rank1 = jnp.sum(jnp.where(hot1, before, 0.0), axis=-1, keepdims=True)
    rank2 = jnp.sum(jnp.where(hot2, before, 0.0), axis=-1, keepdims=True)
    cnt_ref[...] += jnp.sum(hot, axis=0, keepdims=True)
    cnt_out[...] = cnt_ref[...]

    rec = jnp.where(lane == R_E1, e1, 0.0)
    rec = jnp.where(lane == R_E2, e2, rec)
    rec = jnp.where(lane == R_W1, w1, rec)
    rec = jnp.where(lane == R_W2, w2, rec)
    rec = jnp.where(lane == R_RANK1, rank1, rec)
    rec = jnp.where(lane == R_RANK2, rank2, rec)
    r_out[0] = rec


def _wo_router(attn, x, mod, norm_g, w_o, w_router):
    nb, s, d = x.shape
    tm = _tile(s, 256)
    wr = jnp.zeros((d, LANES), F32).at[:, :N_EXPERTS].set(w_router)
    return pl.pallas_call(
        _wo_router_kernel,
        grid=(nb, s // tm),
        in_specs=[
            pl.BlockSpec((1, tm, d), lambda b, i: (b, i, 0)),
            pl.BlockSpec((1, tm, d), lambda b, i: (b, i, 0)),
            pl.BlockSpec((1, 6, d), lambda b, i: (b, 0, 0)),
            pl.BlockSpec((1, d), lambda b, i: (0, 0)),
            pl.BlockSpec((d, d), lambda b, i: (0, 0)),
            pl.BlockSpec((d, LANES), lambda b, i: (0, 0)),
        ],
        out_specs=[
            pl.BlockSpec((1, tm, d), lambda b, i: (b, i, 0)),
            pl.BlockSpec((1, tm, d), lambda b, i: (b, i, 0)),
            pl.BlockSpec((1, tm, LANES), lambda b, i: (b, i, 0)),
            pl.BlockSpec((1, LANES), lambda b, i: (0, 0)),
        ],
        out_shape=[
            jax.ShapeDtypeStruct((nb, s, d), F32),
            jax.ShapeDtypeStruct((nb, s, d), F32),
            jax.ShapeDtypeStruct((nb, s, LANES), F32),
            jax.ShapeDtypeStruct((1, LANES), F32),
        ],
        scratch_shapes=[pltpu.VMEM((1, LANES), F32)],
        compiler_params=_cparams(("arbitrary", "arbitrary")),
        name="wo_router",
    )(attn, x, mod, norm_g.reshape(1, d), w_o, wr)


def _dispatch_kernel(pos_ref, h_ref, xs_in, xs_out, sem, *, td):
    del xs_in

    def row_copy(t, p):
        return pltpu.make_async_copy(h_ref.at[pl.ds(t, 1)], xs_out.at[pl.ds(p, 1)], sem)

    def issue(t, carry):
        row_copy(t, pos_ref[2 * t]).start()
        row_copy(t, pos_ref[2 * t + 1]).start()
        return carry

    lax.fori_loop(0, td, issue, 0)

    def drain(t, carry):
        row_copy(0, 0).wait()
        row_copy(0, 0).wait()
        return carry

    lax.fori_loop(0, td, drain, 0)


def _dispatch(h2, pos, xs_zero):
    n, d = h2.shape
    td = _tile(n, 512)
    return pl.pallas_call(
        functools.partial(_dispatch_kernel, td=td),
        grid=(n // td,),
        in_specs=[
            pl.BlockSpec((2 * td,), lambda i: (i,), memory_space=pltpu.SMEM),
            pl.BlockSpec((td, d), lambda i: (i, 0)),
            pl.BlockSpec(memory_space=pl.ANY),
        ],
        out_specs=pl.BlockSpec(memory_space=pl.ANY),
        out_shape=jax.ShapeDtypeStruct(xs_zero.shape, xs_zero.dtype),
        scratch_shapes=[pltpu.SemaphoreType.DMA(())],
        input_output_aliases={2: 0},
        compiler_params=_cparams(("arbitrary",)),
        name="dispatch",
    )(pos.reshape(-1), h2, xs_zero)


def _moe_kernel(be_ref, bv_ref, xs_ref, wg_ref, wu_ref, wd_ref, o_ref, xb_ref):
    i = pl.program_id(0)
    j = pl.program_id(1)
    valid = bv_ref[i] > 0

    @pl.when(valid & (j == 0))
    def _():
        xb_ref[...] = xs_ref[...].astype(BF16)

    @pl.when(valid)
    def _():
        xb = xb_ref[...]
        gate = jnp.dot(xb, wg_ref[0], preferred_element_type=F32)
        up = jnp.dot(xb, wu_ref[0], preferred_element_type=F32)
        a = ((gate * _sigmoid(gate)) * up).astype(BF16)
        part = jnp.dot(a, wd_ref[0], preferred_element_type=F32)

        @pl.when(j == 0)
        def _():
            o_ref[...] = part

        @pl.when(j > 0)
        def _():
            o_ref[...] += part

    @pl.when(jnp.logical_not(valid) & (j == 0))
    def _():
        o_ref[...] = jnp.zeros_like(o_ref)


def _moe_experts(xs, blk_expert, blk_valid, w_gate, w_up, w_down, tme):
    p, d = xs.shape
    f = w_gate.shape[2]
    tf = _tile(f, 512)
    nf = f // tf

    def fj(i, j, bv):
        return jnp.where(bv[i] > 0, j, nf - 1)

    grid_spec = pltpu.PrefetchScalarGridSpec(
        num_scalar_prefetch=2,
        grid=(p // tme, nf),
        in_specs=[
            pl.BlockSpec((tme, d), lambda i, j, be, bv: (i, 0)),
            pl.BlockSpec((1, d, tf), lambda i, j, be, bv: (be[i], 0, fj(i, j, bv))),
            pl.BlockSpec((1, d, tf), lambda i, j, be, bv: (be[i], 0, fj(i, j, bv))),
            pl.BlockSpec((1, tf, d), lambda i, j, be, bv: (be[i], fj(i, j, bv), 0)),
        ],
        out_specs=pl.BlockSpec((tme, d), lambda i, j, be, bv: (i, 0)),
        scratch_shapes=[pltpu.VMEM((tme, d), BF16)],
    )
    return pl.pallas_call(
        _moe_kernel,
        grid_spec=grid_spec,
        out_shape=jax.ShapeDtypeStruct((p, d), F32),
        compiler_params=_cparams(("arbitrary", "arbitrary")),
        name="moe_experts",
    )(blk_expert, blk_valid, xs, w_gate, w_up, w_down)


def _final_kernel(pos_ref, ys_hbm, x_ref, r_ref, mod_ref, g_ref, o_ref, buf_ref, sem):
    tm = x_ref.shape[1]

    def row_copy(t, k, p):
        return pltpu.make_async_copy(ys_hbm.at[pl.ds(p, 1)], buf_ref.at[k, pl.ds(t, 1)], sem)

    def issue(t, carry):
        row_copy(t, 0, pos_ref[2 * t]).start()
        row_copy(t, 1, pos_ref[2 * t + 1]).start()
        return carry

    lax.fori_loop(0, tm, issue, 0)

    def drain(t, carry):
        row_copy(0, 0, 0).wait()
        row_copy(0, 1, 0).wait()
        return carry

    lax.fori_loop(0, tm, drain, 0)

    rec = r_ref[0]
    w1 = rec[:, R_W1:R_W1 + 1]
    w2 = rec[:, R_W2:R_W2 + 1]
    y = w1 * buf_ref[0] + w2 * buf_ref[1]
    xn = x_ref[0] + mod_ref[0, 5:6, :] * y
    inv = lax.rsqrt(jnp.mean(xn * xn, axis=-1, keepdims=True) + RMS_EPS)
    o_ref[0] = (xn * inv) * g_ref[...]


def _combine_final(ys, pos, x, route, mod, final_g):
    nb, s, d = x.shape
    tm = _tile(s, 512)
    ni = s // tm
    return pl.pallas_call(
        _final_kernel,
        grid=(nb, ni),
        in_specs=[
            pl.BlockSpec((2 * tm,), lambda b, i: (b * ni + i,), memory_space=pltpu.SMEM),
            pl.BlockSpec(memory_space=pl.ANY),
            pl.BlockSpec((1, tm, d), lambda b, i: (b, i, 0)),
            pl.BlockSpec((1, tm, LANES), lambda b, i: (b, i, 0)),
            pl.BlockSpec((1, 6, d), lambda b, i: (b, 0, 0)),
            pl.BlockSpec((1, d), lambda b, i: (0, 0)),
        ],
        out_specs=pl.BlockSpec((1, tm, d), lambda b, i: (b, i, 0)),
        out_shape=jax.ShapeDtypeStruct((nb, s, d), F32),
        scratch_shapes=[pltpu.VMEM((2, tm, d), F32), pltpu.SemaphoreType.DMA(())],
        compiler_params=_cparams(("arbitrary", "arbitrary")),
        name="combine_final",
    )(pos.reshape(-1), ys, x, route, mod, final_g.reshape(1, d))


def _lambda_init(layer_idx):
    return 0.8 - 0.6 * math.exp(-0.3 * layer_idx)


def kernel(x, c, mod_w, mod_b, norm1_g, norm2_g, conv_w_in, conv_b_in, conv_w_dw, conv_b_dw, conv_ln_g, conv_ln_b, conv_w_out, conv_b_out, attn_w_qkv, attn_w_o, attn_lam_q1, attn_lam_k1, attn_lam_q2, attn_lam_k2, attn_subln_g, ffn_w_gate, ffn_w_up, ffn_w_down, moe_w_router, moe_w_gate, moe_w_up, moe_w_down, final_g):
    nb, s, d = x.shape
    n = nb * s
    bf = lambda w: w.astype(BF16)

    mod = _modulation(c, mod_w, mod_b).reshape(mod_w.shape[0], nb, 6, d)

    u = _conv_in(x, mod[0], norm1_g[0], bf(conv_w_in[0]), conv_b_in[0])
    x = _conv_out(u, x, mod[0], conv_w_dw[0], conv_b_dw[0], conv_ln_g[0], conv_ln_b[0],
                  bf(conv_w_out[0]), conv_b_out[0])
    x = _dense_ffn(x, mod[0], norm2_g[0], bf(ffn_w_gate[0]), bf(ffn_w_up[0]), bf(ffn_w_down[0]))

    qkv = _qkv_proj(x, mod[1], norm1_g[1], bf(attn_w_qkv[0]))
    attn = _diff_attention(qkv, attn_lam_q1[0], attn_lam_k1[0], attn_lam_q2[0], attn_lam_k2[0],
                           attn_subln_g[0], _lambda_init(1))
    x, h2, route, counts = _wo_router(attn, x, mod[1], norm2_g[1], bf(attn_w_o[0]),
                                      moe_w_router[0])

    tme = _tile(n, 512)
    nblk = 2 * n // tme + N_EXPERTS
    route2 = route.reshape(n, LANES)
    eidx = route2[:, R_E1:R_E2 + 1].astype(jnp.int32)
    rank = route2[:, R_RANK1:R_RANK2 + 1].astype(jnp.int32)
    cnt = counts[0, :N_EXPERTS].astype(jnp.int32)
    blocks = (cnt + tme - 1) // tme
    blk_end = jnp.cumsum(blocks)
    start = (blk_end - blocks) * tme
    pos = start[eidx] + rank
    total = blk_end[-1]
    bid = jnp.arange(nblk, dtype=jnp.int32)
    blk_valid = (bid < total).astype(jnp.int32)
    blk_expert = jnp.searchsorted(blk_end, jnp.minimum(bid, total - 1), side="right")
    blk_expert = jnp.minimum(blk_expert, N_EXPERTS - 1).astype(jnp.int32)

    xs = _dispatch(h2.reshape(n, d), pos, jnp.zeros((nblk * tme, d), F32))
    ys = _moe_experts(xs, blk_expert, blk_valid, bf(moe_w_gate[0]), bf(moe_w_up[0]),
                      bf(moe_w_down[0]), tme)
    return _combine_final(ys, pos, x, route, mod[1], final_g)
```

```python
import functools
import math

import numpy as np
import jax
import jax.numpy as jnp
from jax import lax
from jax.experimental import pallas as pl
from jax.experimental.pallas import tpu as pltpu

F32 = jnp.float32
BF16 = jnp.bfloat16

RMS_EPS = 1e-6
LN_EPS = 1e-5
CHUNK = 64
CONV_WIDTH = 31
HEAD_DIM = 64
HEAD_LANES = 2 * HEAD_DIM
LOG2E = math.log2(math.e)
QUERY_SCALE = HEAD_DIM ** -0.5 * LOG2E
BF16_EXACT_INT = 256
ATTN_BLOCK = 512
N_EXPERTS = 8
LANES = 128
SUBLANES = 8
CONV_LANES = 512
DMA_UNROLL = 8
HALO = 32
VMEM_LIMIT = 56 * 1024 * 1024


def _tile(n, pref):
    t = min(n, pref)
    while n % t:
        t //= 2
    return t


def _sigmoid(x):
    return 1.0 / (1.0 + jnp.exp(-x))


def _cparams(sem):
    return pltpu.CompilerParams(dimension_semantics=sem, vmem_limit_bytes=VMEM_LIMIT)


def _mod_kernel(c_ref, w_ref, b_ref, o_ref):
    c = c_ref[...]
    ca = (c * _sigmoid(c)).astype(BF16)
    o_ref[0] = jnp.dot(ca, w_ref[0].astype(BF16), preferred_element_type=F32) + b_ref[0]


def _modulation(c, mod_w, mod_b):
    depth, d, n6 = mod_w.shape
    nb = c.shape[0]
    tn = _tile(n6, 1024)
    return pl.pallas_call(
        _mod_kernel,
        grid=(depth, n6 // tn),
        in_specs=[
            pl.BlockSpec((nb, d), lambda l, j: (0, 0)),
            pl.BlockSpec((1, d, tn), lambda l, j: (l, 0, j)),
            pl.BlockSpec((1, 1, tn), lambda l, j: (l, 0, j)),
        ],
        out_specs=pl.BlockSpec((1, nb, tn), lambda l, j: (l, 0, j)),
        out_shape=jax.ShapeDtypeStruct((depth, nb, n6), F32),
        compiler_params=_cparams(("arbitrary", "arbitrary")),
        name="modulation",
    )(c, mod_w, mod_b.reshape(depth, 1, n6))


def _norm_mod_rows(x_ref, g, shift, scale, h_ref, rows, rc):
    one_plus = 1.0 + scale

    def body(r, carry):
        rs = pl.multiple_of(r * rc, rc)
        xv = x_ref[0, pl.ds(rs, rc), :]
        inv = lax.rsqrt(jnp.mean(xv * xv, axis=-1, keepdims=True) + RMS_EPS)
        h = ((xv * inv) * g) * one_plus + shift
        h_ref[pl.ds(rs, rc), :] = h.astype(h_ref.dtype)
        return carry

    lax.fori_loop(0, rows // rc, body, 0)


def _conv_in_kernel(x_ref, mod_ref, g_ref, wv_ref, wg_ref, bv_ref, bg_ref, o_ref, h_ref, *, rc):
    @pl.when(pl.program_id(2) == 0)
    def _():
        _norm_mod_rows(x_ref, g_ref[...], mod_ref[0, 0:1, :], mod_ref[0, 1:2, :], h_ref,
                       h_ref.shape[0], rc)

    h = h_ref[...]
    val = jnp.dot(h, wv_ref[...], preferred_element_type=F32) + bv_ref[...]
    gate = jnp.dot(h, wg_ref[...], preferred_element_type=F32) + bg_ref[...]
    o_ref[0] = (val * _sigmoid(gate)).astype(o_ref.dtype)


def _conv_in(x, mod, norm_g, w_in, b_in):
    nb, s, d = x.shape
    tm = _tile(s, 1024)
    tn = _tile(d, 1024)
    nj = d // tn
    b2 = b_in.reshape(1, 2 * d)
    return pl.pallas_call(
        functools.partial(_conv_in_kernel, rc=_tile(tm, 32)),
        grid=(nb, s // tm, nj),
        in_specs=[
            pl.BlockSpec((1, tm, d), lambda b, i, j: (b, i, 0)),
            pl.BlockSpec((1, 6, d), lambda b, i, j: (b, 0, 0)),
            pl.BlockSpec((1, d), lambda b, i, j: (0, 0)),
            pl.BlockSpec((d, tn), lambda b, i, j: (0, j)),
            pl.BlockSpec((d, tn), lambda b, i, j: (0, j + nj)),
            pl.BlockSpec((1, tn), lambda b, i, j: (0, j)),
            pl.BlockSpec((1, tn), lambda b, i, j: (0, j + nj)),
        ],
        out_specs=pl.BlockSpec((1, tm, tn), lambda b, i, j: (b, i, j)),
        out_shape=jax.ShapeDtypeStruct((nb, s, d), BF16),
        scratch_shapes=[pltpu.VMEM((tm, d), BF16)],
        compiler_params=_cparams(("arbitrary", "arbitrary", "arbitrary")),
        name="conv_in_glu",
    )(x, mod, norm_g.reshape(1, d), w_in, w_in, b2, b2)


def _qkv_kernel(x_ref, mod_ref, g_ref, w_ref, o_ref, h_ref, *, rc, nq):
    @pl.when(pl.program_id(2) == 0)
    def _():
        _norm_mod_rows(x_ref, g_ref[...], mod_ref[0, 0:1, :], mod_ref[0, 1:2, :], h_ref,
                       h_ref.shape[0], rc)

    acc = jnp.dot(h_ref[...], w_ref[...], preferred_element_type=F32)
    is_query = pl.program_id(2) < nq

    @pl.when(is_query)
    def _():
        o_ref[0] = (acc * QUERY_SCALE).astype(o_ref.dtype)

    @pl.when(jnp.logical_not(is_query))
    def _():
        o_ref[0] = acc.astype(o_ref.dtype)


def _qkv_proj(x, mod, norm_g, w_qkv):
    nb, s, d = x.shape
    n3 = w_qkv.shape[1]
    tm = _tile(s, 1024)
    tn = _tile(n3 // 3, 1024)
    return pl.pallas_call(
        functools.partial(_qkv_kernel, rc=_tile(tm, 32), nq=n3 // 3 // tn),
        grid=(nb, s // tm, n3 // tn),
        in_specs=[
            pl.BlockSpec((1, tm, d), lambda b, i, j: (b, i, 0)),
            pl.BlockSpec((1, 6, d), lambda b, i, j: (b, 0, 0)),
            pl.BlockSpec((1, d), lambda b, i, j: (0, 0)),
            pl.BlockSpec((d, tn), lambda b, i, j: (0, j)),
        ],
        out_specs=pl.BlockSpec((1, tm, tn), lambda b, i, j: (b, i, j)),
        out_shape=jax.ShapeDtypeStruct((nb, s, n3), BF16),
        scratch_shapes=[pltpu.VMEM((tm, d), BF16)],
        compiler_params=_cparams(("arbitrary", "arbitrary", "arbitrary")),
        name="qkv_proj",
    )(x, mod, norm_g.reshape(1, d), w_qkv)


def _conv_out_kernel(u_ref, halo_ref, x_ref, mod_ref, wdw_ref, bdw_ref, lng_ref, lnb_ref,
                     wo_ref, bo_ref, o_ref, buf_ref, cv_ref, h_ref, *, rc, cr):
    tm = u_ref.shape[1]
    first = pl.program_id(1) == 0
    halo = halo_ref[0].astype(F32)
    buf_ref[0:HALO, :] = jnp.where(first, jnp.zeros_like(halo), halo)

    def fill(r, carry):
        rs = pl.multiple_of(r * rc, rc)
        buf_ref[pl.ds(HALO + rs, rc), :] = u_ref[0, pl.ds(rs, rc), :].astype(F32)
        return carry

    lax.fori_loop(0, tm // rc, fill, 0)

    lng = lng_ref[...]
    lnb = lnb_ref[...]
    lead = HALO - (CONV_WIDTH - 1)
    d = buf_ref.shape[1]
    cw = _tile(d, CONV_LANES)
    win = cr + HALO

    def taps(r, carry):
        rs = pl.multiple_of(r * cr, cr)
        for lb in range(d // cw):
            cols = slice(lb * cw, (lb + 1) * cw)
            w = buf_ref[pl.ds(rs, win), cols]
            acc = jnp.zeros((cr, cw), F32) + bdw_ref[:, cols]
            for sh in range(SUBLANES):
                ws = w if sh == 0 else pltpu.roll(w, win - sh, axis=0)
                for a in range(win // SUBLANES):
                    k = SUBLANES * a + sh - lead
                    if 0 <= k < CONV_WIDTH:
                        acc = acc + wdw_ref[k:k + 1, cols] * ws[SUBLANES * a:SUBLANES * a + cr]
            cv_ref[pl.ds(rs, cr), cols] = acc
        return carry

    lax.fori_loop(0, tm // cr, taps, 0)

    def conv(r, carry):
        rs = pl.multiple_of(r * rc, rc)
        acc = cv_ref[pl.ds(rs, rc), :]
        mu = jnp.mean(acc, axis=-1, keepdims=True)
        cen = acc - mu
        var = jnp.mean(cen * cen, axis=-1, keepdims=True)
        y = (cen * lax.rsqrt(var + LN_EPS)) * lng + lnb
        h_ref[pl.ds(rs, rc), :] = (y * _sigmoid(y)).astype(h_ref.dtype)
        return carry

    lax.fori_loop(0, tm // rc, conv, 0)

    mix = jnp.dot(h_ref[...], wo_ref[...], preferred_element_type=F32) + bo_ref[...]
    o_ref[0] = x_ref[0] + mod_ref[0, 2:3, :] * mix


def _conv_out(u, x, mod, w_dw, b_dw, ln_g, ln_b, w_out, b_out):
    nb, s, d = x.shape
    tm = _tile(s, 512)
    hb = tm // HALO
    row = lambda a: a.reshape(1, d)
    return pl.pallas_call(
        functools.partial(_conv_out_kernel, rc=_tile(tm, 32), cr=_tile(tm, 32)),
        grid=(nb, s // tm),
        in_specs=[
            pl.BlockSpec((1, tm, d), lambda b, i: (b, i, 0)),
            pl.BlockSpec((1, HALO, d), lambda b, i: (b, jnp.maximum(i * hb - 1, 0), 0)),
            pl.BlockSpec((1, tm, d), lambda b, i: (b, i, 0)),
            pl.BlockSpec((1, 6, d), lambda b, i: (b, 0, 0)),
            pl.BlockSpec((CONV_WIDTH, d), lambda b, i: (0, 0)),
            pl.BlockSpec((1, d), lambda b, i: (0, 0)),
            pl.BlockSpec((1, d), lambda b, i: (0, 0)),
            pl.BlockSpec((1, d), lambda b, i: (0, 0)),
            pl.BlockSpec((d, d), lambda b, i: (0, 0)),
            pl.BlockSpec((1, d), lambda b, i: (0, 0)),
        ],
        out_specs=pl.BlockSpec((1, tm, d), lambda b, i: (b, i, 0)),
        out_shape=jax.ShapeDtypeStruct((nb, s, d), F32),
        scratch_shapes=[pltpu.VMEM((HALO + tm, d), F32), pltpu.VMEM((tm, d), F32),
                        pltpu.VMEM((tm, d), BF16)],
        compiler_params=_cparams(("arbitrary", "arbitrary")),
        name="conv_out",
    )(u, u, x, mod, w_dw, row(b_dw), row(ln_g), row(ln_b), w_out, row(b_out))


def _ffn_kernel(x_ref, mod_ref, g_ref, wg_ref, wu_ref, wd_ref, o_ref, h_ref, *, rc):
    j = pl.program_id(2)

    @pl.when(j == 0)
    def _():
        _norm_mod_rows(x_ref, g_ref[...], mod_ref[0, 3:4, :], mod_ref[0, 4:5, :], h_ref,
                       h_ref.shape[0], rc)

    h = h_ref[...]
    gate = jnp.dot(h, wg_ref[...], preferred_element_type=F32)
    up = jnp.dot(h, wu_ref[...], preferred_element_type=F32)
    a = ((gate * _sigmoid(gate)) * up).astype(BF16)
    part = jnp.dot(a, wd_ref[...], preferred_element_type=F32)

    @pl.when(j == 0)
    def _():
        o_ref[0] = part

    @pl.when(j > 0)
    def _():
        o_ref[0] += part

    @pl.when(j == pl.num_programs(2) - 1)
    def _():
        o_ref[0] = x_ref[0] + mod_ref[0, 5:6, :] * o_ref[0]


def _dense_ffn(x, mod, norm_g, w_gate, w_up, w_down):
    nb, s, d = x.shape
    f = w_gate.shape[1]
    tm = _tile(s, 512)
    tf = _tile(f, 512)
    return pl.pallas_call(
        functools.partial(_ffn_kernel, rc=_tile(tm, 32)),
        grid=(nb, s // tm, f // tf),
        in_specs=[
            pl.BlockSpec((1, tm, d), lambda b, i, j: (b, i, 0)),
            pl.BlockSpec((1, 6, d), lambda b, i, j: (b, 0, 0)),
            pl.BlockSpec((1, d), lambda b, i, j: (0, 0)),
            pl.BlockSpec((d, tf), lambda b, i, j: (0, j)),
            pl.BlockSpec((d, tf), lambda b, i, j: (0, j)),
            pl.BlockSpec((tf, d), lambda b, i, j: (j, 0)),
        ],
        out_specs=pl.BlockSpec((1, tm, d), lambda b, i, j: (b, i, 0)),
        out_shape=jax.ShapeDtypeStruct((nb, s, d), F32),
        scratch_shapes=[pltpu.VMEM((tm, d), BF16)],
        compiler_params=_cparams(("arbitrary", "arbitrary", "arbitrary")),
        name="dense_ffn",
    )(x, mod, norm_g.reshape(1, d), w_gate, w_up, w_down)


def _attn_kernel(slopes_ref, q_ref, k_ref, v_ref, dg_ref, lq1_ref, lk1_ref, lq2_ref, lk2_ref,
                 sg_ref, o_ref, vt_ref, ta_ref, tb_ref, m_ref, l_ref, acc_ref, *, lambda_init):
    tq = q_ref.shape[1]
    s_len = k_ref.shape[1]
    head = pl.program_id(1)
    qi = pl.program_id(2)
    slope = jnp.full((1, LANES), slopes_ref[head], F32)

    @pl.when(qi == 0)
    def _():
        for c in range(s_len // tq):
            blk = v_ref[0, c * tq:(c + 1) * tq, :].astype(F32)
            vt_ref[:, c * tq:(c + 1) * tq] = blk.T.astype(BF16)

    lam = (jnp.exp(jnp.sum(lq1_ref[...] * lk1_ref[...], axis=-1, keepdims=True))
           - jnp.exp(jnp.sum(lq2_ref[...] * lk2_ref[...], axis=-1, keepdims=True))
           + lambda_init)

    q = q_ref[0]
    lane = lax.broadcasted_iota(jnp.int32, q.shape, 1)
    q1 = jnp.where(lane < HEAD_DIM, q, jnp.zeros_like(q))
    q2 = jnp.where(lane >= HEAD_DIM, q, jnp.zeros_like(q))
    nt = (((1,), (1,)), ((), ()))
    q0 = qi * tq

    s_hi = slope.astype(BF16).astype(F32)
    s_mid = (slope - s_hi).astype(BF16).astype(F32)
    s_lo = ((slope - s_hi) - s_mid).astype(BF16).astype(F32)
    alane = lax.broadcasted_iota(jnp.int32, (2 * tq, LANES), 1)
    piece = jnp.where(alane % 3 == 0, s_hi, jnp.where(alane % 3 == 1, s_mid, s_lo))
    q_aug = jnp.where(alane < 6, piece, 0.0).astype(BF16)
    qq = jnp.concatenate([jnp.concatenate([q1, q2], axis=0), q_aug], axis=1)
    krow = lax.broadcasted_iota(jnp.int32, (tq, LANES), 0)
    klane = lax.broadcasted_iota(jnp.int32, (tq, LANES), 1)
    j_split = jnp.where(klane < 3, krow % BF16_EXACT_INT, (krow // BF16_EXACT_INT) * BF16_EXACT_INT)
    k_aug = jnp.where(klane < 6, j_split, 0).astype(F32).astype(BF16)

    def scores(c, t_ref):
        ks = pl.multiple_of(c * tq, tq)
        kc = jnp.concatenate([k_ref[0, pl.ds(ks, tq), :], k_aug], axis=1)
        t_ref[...] = lax.dot_general(kc, qq, nt, preferred_element_type=F32)

    def absorb(c, t_ref, bias, shift):
        ks = pl.multiple_of(c * tq, tq)
        t = t_ref[...] if bias is None else t_ref[...] + bias
        m = m_ref[...]
        m_new = jnp.maximum(m, jnp.max(t, axis=0, keepdims=True) + shift)
        p = jnp.exp2(t - (m_new - shift))
        alpha = jnp.exp2(m - m_new)
        l_ref[...] = alpha * l_ref[...] + jnp.sum(p, axis=0, keepdims=True)
        acc_ref[...] = alpha * acc_ref[...] + jnp.dot(vt_ref[:, pl.ds(ks, tq)], p.astype(BF16),
                                                      preferred_element_type=F32)
        m_ref[...] = m_new

    def absorb_past(c, t_ref):
        absorb(c, t_ref, None, slope[:, 0:1] * (c * tq - q0).astype(F32))

    def absorb_diag(t_ref):
        diag = slope[:, 0:1] * dg_ref[...]
        absorb(qi, t_ref, jnp.concatenate([diag, diag], axis=1), jnp.zeros((1, 1), F32))

    m_ref[...] = jnp.full(m_ref.shape, -jnp.inf, F32)
    l_ref[...] = jnp.zeros(l_ref.shape, F32)
    acc_ref[...] = jnp.zeros(acc_ref.shape, F32)
    scores(0, ta_ref)

    def pair(p, carry):
        c = 2 * p
        scores(c + 1, tb_ref)
        absorb_past(c, ta_ref)
        scores(c + 2, ta_ref)
        absorb_past(c + 1, tb_ref)
        return carry

    lax.fori_loop(0, qi // 2, pair, 0)
    odd = qi % 2 == 1

    @pl.when(odd)
    def _():
        scores(qi, tb_ref)
        absorb_past(qi - 1, ta_ref)
        absorb_diag(tb_ref)

    @pl.when(jnp.logical_not(odd))
    def _():
        absorb_diag(ta_ref)

    acc = acc_ref[...]
    l = l_ref[...]
    o = acc[:, :tq] / l[:, :tq] - lam * (acc[:, tq:] / l[:, tq:])
    o = o * lax.rsqrt(jnp.mean(o * o, axis=0, keepdims=True) + RMS_EPS)
    o_ref[0] = ((o.T * sg_ref[...]) * (1.0 - lambda_init)).astype(o_ref.dtype)


def _diag_bias(tq):
    j = np.arange(tq)[:, None]
    r = np.arange(tq)[None, :]
    tile = (np.minimum(j, 2 * r - j) - j).astype(np.float32)
    return jnp.asarray(np.where(j // CHUNK <= r // CHUNK, tile, -np.inf).astype(np.float32))


def _diff_attention(qkv, lam_q1, lam_k1, lam_q2, lam_k2, subln_g, lambda_init):
    nb, s, n3 = qkv.shape
    d = n3 // 3
    heads = d // HEAD_LANES
    tq = _tile(s, ATTN_BLOCK)
    ratio = 2.0 ** (-8.0 / heads)
    slopes = np.array([ratio ** (h + 1) for h in range(heads)], dtype=np.float32)
    slopes2 = jnp.asarray(slopes * np.float32(LOG2E))
    vec = lambda a: a.reshape(1, HEAD_DIM)
    small = pl.BlockSpec((1, HEAD_DIM), lambda b, h, i: (0, 0))
    return pl.pallas_call(
        functools.partial(_attn_kernel, lambda_init=lambda_init),
        grid=(nb, heads, s // tq),
        in_specs=[
            pl.BlockSpec(memory_space=pltpu.SMEM),
            pl.BlockSpec((1, tq, HEAD_LANES), lambda b, h, i: (b, i, h)),
            pl.BlockSpec((1, s, HEAD_LANES), lambda b, h, i: (b, 0, heads + h)),
            pl.BlockSpec((1, s, HEAD_LANES), lambda b, h, i: (b, 0, 2 * heads + h)),
            pl.BlockSpec((tq, tq), lambda b, h, i: (0, 0)),
            small, small, small, small,
            pl.BlockSpec((1, HEAD_LANES), lambda b, h, i: (0, 0)),
        ],
        out_specs=pl.BlockSpec((1, tq, HEAD_LANES), lambda b, h, i: (b, i, h)),
        out_shape=jax.ShapeDtypeStruct((nb, s, d), BF16),
        scratch_shapes=[pltpu.VMEM((HEAD_LANES, s), BF16),
                        pltpu.VMEM((tq, 2 * tq), F32), pltpu.VMEM((tq, 2 * tq), F32),
                        pltpu.VMEM((1, 2 * tq), F32), pltpu.VMEM((1, 2 * tq), F32),
                        pltpu.VMEM((HEAD_LANES, 2 * tq), F32)],
        compiler_params=_cparams(("arbitrary", "arbitrary", "arbitrary")),
        name="diff_attention",
    )(slopes2, qkv, qkv, qkv, _diag_bias(tq), vec(lam_q1), vec(lam_k1), vec(lam_q2), vec(lam_k2),
      subln_g.reshape(1, HEAD_LANES))


R_E1, R_E2, R_W1, R_W2, R_RANK1, R_RANK2 = range(6)


def _wo_router_kernel(a_ref, x_ref, mod_ref, g_ref, wo_ref, wr_ref, x_out, h_out, r_out, cnt_out,
                      cnt_ref):
    tm = a_ref.shape[1]

    @pl.when((pl.program_id(0) == 0) & (pl.program_id(1) == 0))
    def _():
        cnt_ref[...] = jnp.zeros_like(cnt_ref)

    mix = jnp.dot(a_ref[0], wo_ref[...], preferred_element_type=F32)
    xn = x_ref[0] + mod_ref[0, 2:3, :] * mix
    x_out[0] = xn
    inv = lax.rsqrt(jnp.mean(xn * xn, axis=-1, keepdims=True) + RMS_EPS)
    h = ((xn * inv) * g_ref[...]) * (1.0 + mod_ref[0, 4:5, :]) + mod_ref[0, 3:4, :]
    h_out[0] = h

    lane = lax.broadcasted_iota(jnp.int32, (tm, LANES), 1).astype(F32)
    logits = jnp.full((tm, LANES), -jnp.inf, F32)
    for e in range(N_EXPERTS):
        col = jnp.sum(h * wr_ref[e:e + 1, :], axis=-1, keepdims=True)
        logits = jnp.where(lane == e, col, logits)
    v1 = jnp.max(logits, axis=-1, keepdims=True)
    e1 = jnp.min(jnp.where(logits == v1, lane, float(LANES)), axis=-1, keepdims=True)
    hot1 = lane == e1
    rest = jnp.where(hot1, -jnp.inf, logits)
    v2 = jnp.max(rest, axis=-1, keepdims=True)
    e2 = jnp.min(jnp.where(rest == v2, lane, float(LANES)), axis=-1, keepdims=True)
    hot2 = lane == e2
    ex = jnp.exp(v2 - v1)
    w1 = 1.0 / (1.0 + ex)
    w2 = ex / (1.0 + ex)

    hot = jnp.where(hot1 | hot2, 1.0, 0.0)
    ri = lax.broadcasted_iota(jnp.int32, (tm, tm), 0)
    ci = lax.broadcasted_iota(jnp.int32, (tm, tm), 1)
    tri = jnp.where(ci < ri, 1.0, 0.0).astype(BF16)
    before = jnp.dot(tri, hot.astype(BF16), preferred_element_type=F32) + cnt_ref[...]
    rank1 = jnp.sum(jnp.where(hot1, before, 0.0), axis=-1, keepdims=True)
    rank2 = jnp.sum(jnp.where(hot2, before, 0.0), axis=-1, keepdims=True)
    cnt_ref[...] += jnp.sum(hot, axis=0, keepdims=True)
    cnt_out[...] = cnt_ref[...]

    rec = jnp.where(lane == R_E1, e1, 0.0)
    rec = jnp.where(lane == R_E2, e2, rec)
    rec = jnp.where(lane == R_W1, w1, rec)
    rec = jnp.where(lane == R_W2, w2, rec)
    rec = jnp.where(lane == R_RANK1, rank1, rec)
    rec = jnp.where(lane == R_RANK2, rank2, rec)
    r_out[0] = rec


def _wo_router(attn, x, mod, norm_g, w_o, w_router):
    nb, s, d = x.shape
    tm = _tile(s, 512)
    wr = w_router.T
    return pl.pallas_call(
        _wo_router_kernel,
        grid=(nb, s // tm),
        in_specs=[
            pl.BlockSpec((1, tm, d), lambda b, i: (b, i, 0)),
            pl.BlockSpec((1, tm, d), lambda b, i: (b, i, 0)),
            pl.BlockSpec((1, 6, d), lambda b, i: (b, 0, 0)),
            pl.BlockSpec((1, d), lambda b, i: (0, 0)),
            pl.BlockSpec((d, d), lambda b, i: (0, 0)),
            pl.BlockSpec((N_EXPERTS, d), lambda b, i: (0, 0)),
        ],
        out_specs=[
            pl.BlockSpec((1, tm, d), lambda b, i: (b, i, 0)),
            pl.BlockSpec((1, tm, d), lambda b, i: (b, i, 0)),
            pl.BlockSpec((1, tm, LANES), lambda b, i: (b, i, 0)),
            pl.BlockSpec((1, LANES), lambda b, i: (0, 0)),
        ],
        out_shape=[
            jax.ShapeDtypeStruct((nb, s, d), F32),
            jax.ShapeDtypeStruct((nb, s, d), F32),
            jax.ShapeDtypeStruct((nb, s, LANES), F32),
            jax.ShapeDtypeStruct((1, LANES), F32),
        ],
        scratch_shapes=[pltpu.VMEM((1, LANES), F32)],
        compiler_params=_cparams(("arbitrary", "arbitrary")),
        name="wo_router",
    )(attn, x, mod, norm_g.reshape(1, d), w_o, wr)


def _dispatch_kernel(pos_ref, h_ref, xs_in, xs_out, sem, *, td):
    del xs_in

    def row_copy(t, p):
        return pltpu.make_async_copy(h_ref.at[pl.ds(t, 1)], xs_out.at[pl.ds(p, 1)], sem)

    def issue(t, carry):
        row_copy(t, pos_ref[2 * t]).start()
        row_copy(t, pos_ref[2 * t + 1]).start()
        return carry

    lax.fori_loop(0, td, issue, 0, unroll=DMA_UNROLL)

    def drain(t, carry):
        row_copy(0, 0).wait()
        row_copy(0, 0).wait()
        return carry

    lax.fori_loop(0, td, drain, 0, unroll=DMA_UNROLL)


def _dispatch(h2, pos, xs_zero):
    n, d = h2.shape
    td = _tile(n, 512)
    return pl.pallas_call(
        functools.partial(_dispatch_kernel, td=td),
        grid=(n // td,),
        in_specs=[
            pl.BlockSpec((2 * td,), lambda i: (i,), memory_space=pltpu.SMEM),
            pl.BlockSpec((td, d), lambda i: (i, 0)),
            pl.BlockSpec(memory_space=pl.ANY),
        ],
        out_specs=pl.BlockSpec(memory_space=pl.ANY),
        out_shape=jax.ShapeDtypeStruct(xs_zero.shape, xs_zero.dtype),
        scratch_shapes=[pltpu.SemaphoreType.DMA(())],
        input_output_aliases={2: 0},
        compiler_params=_cparams(("arbitrary",)),
        name="dispatch",
    )(pos.reshape(-1), h2, xs_zero)


def _moe_kernel(be_ref, bv_ref, xs_ref, wg_ref, wu_ref, wd_ref, o_ref, xb_ref):
    i = pl.program_id(0)
    j = pl.program_id(1)
    valid = bv_ref[i] > 0

    @pl.when(valid & (j == 0))
    def _():
        xb_ref[...] = xs_ref[...].astype(BF16)

    @pl.when(valid)
    def _():
        xb = xb_ref[...]
        gate = jnp.dot(xb, wg_ref[0], preferred_element_type=F32)
        up = jnp.dot(xb, wu_ref[0], preferred_element_type=F32)
        a = ((gate * _sigmoid(gate)) * up).astype(BF16)
        part = jnp.dot(a, wd_ref[0], preferred_element_type=F32)

        @pl.when(j == 0)
        def _():
            o_ref[...] = part

        @pl.when(j > 0)
        def _():
            o_ref[...] += part

    @pl.when(jnp.logical_not(valid) & (j == 0))
    def _():
        o_ref[...] = jnp.zeros_like(o_ref)


def _moe_experts(xs, blk_expert, blk_valid, w_gate, w_up, w_down, tme):
    p, d = xs.shape
    f = w_gate.shape[2]
    tf = _tile(f, 1024)
    nf = f // tf

    def fj(i, j, bv):
        return jnp.where(bv[i] > 0, j, nf - 1)

    grid_spec = pltpu.PrefetchScalarGridSpec(
        num_scalar_prefetch=2,
        grid=(p // tme, nf),
        in_specs=[
            pl.BlockSpec((tme, d), lambda i, j, be, bv: (i, 0)),
            pl.BlockSpec((1, d, tf), lambda i, j, be, bv: (be[i], 0, fj(i, j, bv))),
            pl.BlockSpec((1, d, tf), lambda i, j, be, bv: (be[i], 0, fj(i, j, bv))),
            pl.BlockSpec((1, tf, d), lambda i, j, be, bv: (be[i], fj(i, j, bv), 0)),
        ],
        out_specs=pl.BlockSpec((tme, d), lambda i, j, be, bv: (i, 0)),
        scratch_shapes=[pltpu.VMEM((tme, d), BF16)],
    )
    return pl.pallas_call(
        _moe_kernel,
        grid_spec=grid_spec,
        out_shape=jax.ShapeDtypeStruct((p, d), F32),
        compiler_params=_cparams(("arbitrary", "arbitrary")),
        name="moe_experts",
    )(blk_expert, blk_valid, xs, w_gate, w_up, w_down)


def _final_kernel(pos_ref, ys_hbm, x_ref, r_ref, mod_ref, g_ref, o_ref, buf_ref, sem):
    tm = x_ref.shape[1]

    def row_copy(t, k, p):
        return pltpu.make_async_copy(ys_hbm.at[pl.ds(p, 1)], buf_ref.at[k, pl.ds(t, 1)], sem)

    def issue(t, carry):
        row_copy(t, 0, pos_ref[2 * t]).start()
        row_copy(t, 1, pos_ref[2 * t + 1]).start()
        return carry

    lax.fori_loop(0, tm, issue, 0, unroll=DMA_UNROLL)

    def drain(t, carry):
        row_copy(0, 0, 0).wait()
        row_copy(0, 1, 0).wait()
        return carry

    lax.fori_loop(0, tm, drain, 0, unroll=DMA_UNROLL)

    rec = r_ref[0]
    w1 = rec[:, R_W1:R_W1 + 1]
    w2 = rec[:, R_W2:R_W2 + 1]
    y = w1 * buf_ref[0] + w2 * buf_ref[1]
    xn = x_ref[0] + mod_ref[0, 5:6, :] * y
    inv = lax.rsqrt(jnp.mean(xn * xn, axis=-1, keepdims=True) + RMS_EPS)
    o_ref[0] = (xn * inv) * g_ref[...]


def _combine_final(ys, pos, x, route, mod, final_g):
    nb, s, d = x.shape
    tm = _tile(s, 512)
    ni = s // tm
    return pl.pallas_call(
        _final_kernel,
        grid=(nb, ni),
        in_specs=[
            pl.BlockSpec((2 * tm,), lambda b, i: (b * ni + i,), memory_space=pltpu.SMEM),
            pl.BlockSpec(memory_space=pl.ANY),
            pl.BlockSpec((1, tm, d), lambda b, i: (b, i, 0)),
            pl.BlockSpec((1, tm, LANES), lambda b, i: (b, i, 0)),
            pl.BlockSpec((1, 6, d), lambda b, i: (b, 0, 0)),
            pl.BlockSpec((1, d), lambda b, i: (0, 0)),
        ],
        out_specs=pl.BlockSpec((1, tm, d), lambda b, i: (b, i, 0)),
        out_shape=jax.ShapeDtypeStruct((nb, s, d), F32),
        scratch_shapes=[pltpu.VMEM((2, tm, d), F32), pltpu.SemaphoreType.DMA(())],
        compiler_params=_cparams(("arbitrary", "arbitrary")),
        name="combine_final",
    )(pos.reshape(-1), ys, x, route, mod, final_g.reshape(1, d))


def _lambda_init(layer_idx):
    return 0.8 - 0.6 * math.exp(-0.3 * layer_idx)


def kernel(x, c, mod_w, mod_b, norm1_g, norm2_g, conv_w_in, conv_b_in, conv_w_dw, conv_b_dw, conv_ln_g, conv_ln_b, conv_w_out, conv_b_out, attn_w_qkv, attn_w_o, attn_lam_q1, attn_lam_k1, attn_lam_q2, attn_lam_k2, attn_subln_g, ffn_w_gate, ffn_w_up, ffn_w_down, moe_w_router, moe_w_gate, moe_w_up, moe_w_down, final_g):
    nb, s, d = x.shape
    n = nb * s
    bf = lambda w: w.astype(BF16)

    mod = _modulation(c, mod_w, mod_b).reshape(mod_w.shape[0], nb, 6, d)

    u = _conv_in(x, mod[0], norm1_g[0], bf(conv_w_in[0]), conv_b_in[0])
    x = _conv_out(u, x, mod[0], conv_w_dw[0], conv_b_dw[0], conv_ln_g[0], conv_ln_b[0],
                  bf(conv_w_out[0]), conv_b_out[0])
    x = _dense_ffn(x, mod[0], norm2_g[0], bf(ffn_w_gate[0]), bf(ffn_w_up[0]), bf(ffn_w_down[0]))

    qkv = _qkv_proj(x, mod[1], norm1_g[1], bf(attn_w_qkv[0]))
    attn = _diff_attention(qkv, attn_lam_q1[0], attn_lam_k1[0], attn_lam_q2[0], attn_lam_k2[0],
                           attn_subln_g[0], _lambda_init(1))
    x, h2, route, counts = _wo_router(attn, x, mod[1], norm2_g[1], bf(attn_w_o[0]),
                                      moe_w_router[0])

    tme = _tile(n, 512)
    nblk = 2 * n // tme + N_EXPERTS
    route2 = route.reshape(n, LANES)
    eidx = route2[:, R_E1:R_E2 + 1].astype(jnp.int32)
    rank = route2[:, R_RANK1:R_RANK2 + 1].astype(jnp.int32)
    cnt = counts[0, :N_EXPERTS].astype(jnp.int32)
    blocks = (cnt + tme - 1) // tme
    blk_end = jnp.cumsum(blocks)
    start = (blk_end - blocks) * tme
    pos = start[eidx] + rank
    total = blk_end[-1]
    bid = jnp.arange(nblk, dtype=jnp.int32)
    blk_valid = (bid < total).astype(jnp.int32)
    blk_expert = jnp.searchsorted(blk_end, jnp.minimum(bid, total - 1), side="right")
    blk_expert = jnp.minimum(blk_expert, N_EXPERTS - 1).astype(jnp.int32)

    xs = _dispatch(h2.reshape(n, d), pos, jnp.zeros((nblk * tme, d), F32))
    ys = _moe_experts(xs, blk_expert, blk_valid, bf(moe_w_gate[0]), bf(moe_w_up[0]),
                      bf(moe_w_down[0]), tme)
    return _combine_final(ys, pos, x, route, mod[1], final_g)
```

```python
import functools
import math

import numpy as np
import jax
import jax.numpy as jnp
from jax import lax
from jax.experimental import pallas as pl
from jax.experimental.pallas import tpu as pltpu

F32 = jnp.float32
BF16 = jnp.bfloat16

RMS_EPS = 1e-6
LN_EPS = 1e-5
CHUNK = 64
CONV_WIDTH = 31
HEAD_DIM = 64
HEAD_LANES = 2 * HEAD_DIM
LOG2E = math.log2(math.e)
QUERY_SCALE = HEAD_DIM ** -0.5 * LOG2E
BF16_EXACT_INT = 256
ATTN_BLOCK = 512
N_EXPERTS = 8
LANES = 128
SUBLANES = 8
CONV_LANES = 512
DMA_UNROLL = 8
HALO = 32
VMEM_LIMIT = 56 * 1024 * 1024


def _tile(n, pref):
    t = min(n, pref)
    while n % t:
        t //= 2
    return t


def _sigmoid(x):
    return 1.0 / (1.0 + jnp.exp(-x))


def _cparams(sem):
    return pltpu.CompilerParams(dimension_semantics=sem, vmem_limit_bytes=VMEM_LIMIT)


def _mod_kernel(c_ref, w_ref, b_ref, o_ref):
    c = c_ref[...]
    ca = (c * _sigmoid(c)).astype(BF16)
    o_ref[0] = jnp.dot(ca, w_ref[0].astype(BF16), preferred_element_type=F32) + b_ref[0]


def _modulation(c, mod_w, mod_b):
    depth, d, n6 = mod_w.shape
    nb = c.shape[0]
    tn = _tile(n6, 1024)
    return pl.pallas_call(
        _mod_kernel,
        grid=(depth, n6 // tn),
        in_specs=[
            pl.BlockSpec((nb, d), lambda l, j: (0, 0)),
            pl.BlockSpec((1, d, tn), lambda l, j: (l, 0, j)),
            pl.BlockSpec((1, 1, tn), lambda l, j: (l, 0, j)),
        ],
        out_specs=pl.BlockSpec((1, nb, tn), lambda l, j: (l, 0, j)),
        out_shape=jax.ShapeDtypeStruct((depth, nb, n6), F32),
        compiler_params=_cparams(("arbitrary", "arbitrary")),
        name="modulation",
    )(c, mod_w, mod_b.reshape(depth, 1, n6))


def _norm_mod_rows(x_ref, g, shift, scale, h_ref, rows, rc):
    one_plus = 1.0 + scale

    def body(r, carry):
        rs = pl.multiple_of(r * rc, rc)
        xv = x_ref[0, pl.ds(rs, rc), :]
        inv = lax.rsqrt(jnp.mean(xv * xv, axis=-1, keepdims=True) + RMS_EPS)
        h = ((xv * inv) * g) * one_plus + shift
        h_ref[pl.ds(rs, rc), :] = h.astype(h_ref.dtype)
        return carry

    lax.fori_loop(0, rows // rc, body, 0, unroll=2)


def _conv_in_kernel(x_ref, mod_ref, g_ref, wv_ref, wg_ref, bv_ref, bg_ref, o_ref, h_ref, *, rc):
    @pl.when(pl.program_id(2) == 0)
    def _():
        _norm_mod_rows(x_ref, g_ref[...], mod_ref[0, 0:1, :], mod_ref[0, 1:2, :], h_ref,
                       h_ref.shape[0], rc)

    h = h_ref[...]
    val = jnp.dot(h, wv_ref[...], preferred_element_type=F32) + bv_ref[...]
    gate = jnp.dot(h, wg_ref[...], preferred_element_type=F32) + bg_ref[...]
    o_ref[0] = (val * _sigmoid(gate)).astype(o_ref.dtype)


def _conv_in(x, mod, norm_g, w_in, b_in):
    nb, s, d = x.shape
    tm = _tile(s, 1024)
    tn = _tile(d, 1024)
    nj = d // tn
    b2 = b_in.reshape(1, 2 * d)
    return pl.pallas_call(
        functools.partial(_conv_in_kernel, rc=_tile(tm, 32)),
        grid=(nb, s // tm, nj),
        in_specs=[
            pl.BlockSpec((1, tm, d), lambda b, i, j: (b, i, 0)),
            pl.BlockSpec((1, 6, d), lambda b, i, j: (b, 0, 0)),
            pl.BlockSpec((1, d), lambda b, i, j: (0, 0)),
            pl.BlockSpec((d, tn), lambda b, i, j: (0, j)),
            pl.BlockSpec((d, tn), lambda b, i, j: (0, j + nj)),
            pl.BlockSpec((1, tn), lambda b, i, j: (0, j)),
            pl.BlockSpec((1, tn), lambda b, i, j: (0, j + nj)),
        ],
        out_specs=pl.BlockSpec((1, tm, tn), lambda b, i, j: (b, i, j)),
        out_shape=jax.ShapeDtypeStruct((nb, s, d), BF16),
        scratch_shapes=[pltpu.VMEM((tm, d), BF16)],
        compiler_params=_cparams(("arbitrary", "arbitrary", "arbitrary")),
        name="conv_in_glu",
    )(x, mod, norm_g.reshape(1, d), w_in, w_in, b2, b2)


def _qkv_kernel(x_ref, mod_ref, g_ref, w_ref, o_ref, h_ref, *, rc, nq):
    @pl.when(pl.program_id(2) == 0)
    def _():
        _norm_mod_rows(x_ref, g_ref[...], mod_ref[0, 0:1, :], mod_ref[0, 1:2, :], h_ref,
                       h_ref.shape[0], rc)

    acc = jnp.dot(h_ref[...], w_ref[...], preferred_element_type=F32)
    is_query = pl.program_id(2) < nq

    @pl.when(is_query)
    def _():
        o_ref[0] = (acc * QUERY_SCALE).astype(o_ref.dtype)

    @pl.when(jnp.logical_not(is_query))
    def _():
        o_ref[0] = acc.astype(o_ref.dtype)


def _qkv_proj(x, mod, norm_g, w_qkv):
    nb, s, d = x.shape
    n3 = w_qkv.shape[1]
    tm = _tile(s, 1024)
    tn = _tile(n3 // 3, 1024)
    return pl.pallas_call(
        functools.partial(_qkv_kernel, rc=_tile(tm, 32), nq=n3 // 3 // tn),
        grid=(nb, s // tm, n3 // tn),
        in_specs=[
            pl.BlockSpec((1, tm, d), lambda b, i, j: (b, i, 0)),
            pl.BlockSpec((1, 6, d), lambda b, i, j: (b, 0, 0)),
            pl.BlockSpec((1, d), lambda b, i, j: (0, 0)),
            pl.BlockSpec((d, tn), lambda b, i, j: (0, j)),
        ],
        out_specs=pl.BlockSpec((1, tm, tn), lambda b, i, j: (b, i, j)),
        out_shape=jax.ShapeDtypeStruct((nb, s, n3), BF16),
        scratch_shapes=[pltpu.VMEM((tm, d), BF16)],
        compiler_params=_cparams(("arbitrary", "arbitrary", "arbitrary")),
        name="qkv_proj",
    )(x, mod, norm_g.reshape(1, d), w_qkv)


def _conv_out_kernel(u_ref, halo_ref, x_ref, mod_ref, wdw_ref, bdw_ref, lng_ref, lnb_ref,
                     wo_ref, bo_ref, o_ref, buf_ref, cv_ref, h_ref, *, rc, cr):
    tm = u_ref.shape[1]
    first = pl.program_id(1) == 0
    halo = halo_ref[0].astype(F32)
    buf_ref[0:HALO, :] = jnp.where(first, jnp.zeros_like(halo), halo)

    def fill(r, carry):
        rs = pl.multiple_of(r * rc, rc)
        buf_ref[pl.ds(HALO + rs, rc), :] = u_ref[0, pl.ds(rs, rc), :].astype(F32)
        return carry

    lax.fori_loop(0, tm // rc, fill, 0)

    lng = lng_ref[...]
    lnb = lnb_ref[...]
    lead = HALO - (CONV_WIDTH - 1)
    d = buf_ref.shape[1]
    cw = _tile(d, CONV_LANES)
    win = cr + HALO

    def taps(r, carry):
        rs = pl.multiple_of(r * cr, cr)
        for lb in range(d // cw):
            cols = slice(lb * cw, (lb + 1) * cw)
            w = buf_ref[pl.ds(rs, win), cols]
            acc = jnp.zeros((cr, cw), F32) + bdw_ref[:, cols]
            for sh in range(SUBLANES):
                ws = w if sh == 0 else pltpu.roll(w, win - sh, axis=0)
                for a in range(win // SUBLANES):
                    k = SUBLANES * a + sh - lead
                    if 0 <= k < CONV_WIDTH:
                        acc = acc + wdw_ref[k:k + 1, cols] * ws[SUBLANES * a:SUBLANES * a + cr]
            cv_ref[pl.ds(rs, cr), cols] = acc
        return carry

    lax.fori_loop(0, tm // cr, taps, 0)

    def conv(r, carry):
        rs = pl.multiple_of(r * rc, rc)
        acc = cv_ref[pl.ds(rs, rc), :]
        mu = jnp.mean(acc, axis=-1, keepdims=True)
        cen = acc - mu
        var = jnp.mean(cen * cen, axis=-1, keepdims=True)
        y = (cen * lax.rsqrt(var + LN_EPS)) * lng + lnb
        h_ref[pl.ds(rs, rc), :] = (y * _sigmoid(y)).astype(h_ref.dtype)
        return carry

    lax.fori_loop(0, tm // rc, conv, 0, unroll=2)

    mix = jnp.dot(h_ref[...], wo_ref[...], preferred_element_type=F32) + bo_ref[...]
    o_ref[0] = x_ref[0] + mod_ref[0, 2:3, :] * mix


def _conv_out(u, x, mod, w_dw, b_dw, ln_g, ln_b, w_out, b_out):
    nb, s, d = x.shape
    tm = _tile(s, 512)
    hb = tm // HALO
    row = lambda a: a.reshape(1, d)
    return pl.pallas_call(
        functools.partial(_conv_out_kernel, rc=_tile(tm, 32), cr=_tile(tm, 64)),
        grid=(nb, s // tm),
        in_specs=[
            pl.BlockSpec((1, tm, d), lambda b, i: (b, i, 0)),
            pl.BlockSpec((1, HALO, d), lambda b, i: (b, jnp.maximum(i * hb - 1, 0), 0)),
            pl.BlockSpec((1, tm, d), lambda b, i: (b, i, 0)),
            pl.BlockSpec((1, 6, d), lambda b, i: (b, 0, 0)),
            pl.BlockSpec((CONV_WIDTH, d), lambda b, i: (0, 0)),
            pl.BlockSpec((1, d), lambda b, i: (0, 0)),
            pl.BlockSpec((1, d), lambda b, i: (0, 0)),
            pl.BlockSpec((1, d), lambda b, i: (0, 0)),
            pl.BlockSpec((d, d), lambda b, i: (0, 0)),
            pl.BlockSpec((1, d), lambda b, i: (0, 0)),
        ],
        out_specs=pl.BlockSpec((1, tm, d), lambda b, i: (b, i, 0)),
        out_shape=jax.ShapeDtypeStruct((nb, s, d), F32),
        scratch_shapes=[pltpu.VMEM((HALO + tm, d), F32), pltpu.VMEM((tm, d), F32),
                        pltpu.VMEM((tm, d), BF16)],
        compiler_params=_cparams(("arbitrary", "arbitrary")),
        name="conv_out",
    )(u, u, x, mod, w_dw, row(b_dw), row(ln_g), row(ln_b), w_out, row(b_out))


def _ffn_kernel(x_ref, mod_ref, g_ref, wg_ref, wu_ref, wd_ref, o_ref, h_ref, *, rc):
    j = pl.program_id(2)

    @pl.when(j == 0)
    def _():
        _norm_mod_rows(x_ref, g_ref[...], mod_ref[0, 3:4, :], mod_ref[0, 4:5, :], h_ref,
                       h_ref.shape[0], rc)

    h = h_ref[...]
    gate = jnp.dot(h, wg_ref[...], preferred_element_type=F32)
    up = jnp.dot(h, wu_ref[...], preferred_element_type=F32)
    a = ((gate * _sigmoid(gate)) * up).astype(BF16)
    part = jnp.dot(a, wd_ref[...], preferred_element_type=F32)

    @pl.when(j == 0)
    def _():
        o_ref[0] = part

    @pl.when(j > 0)
    def _():
        o_ref[0] += part

    @pl.when(j == pl.num_programs(2) - 1)
    def _():
        o_ref[0] = x_ref[0] + mod_ref[0, 5:6, :] * o_ref[0]


def _dense_ffn(x, mod, norm_g, w_gate, w_up, w_down):
    nb, s, d = x.shape
    f = w_gate.shape[1]
    tm = _tile(s, 1024)
    tf = _tile(f, 512)
    return pl.pallas_call(
        functools.partial(_ffn_kernel, rc=_tile(tm, 32)),
        grid=(nb, s // tm, f // tf),
        in_specs=[
            pl.BlockSpec((1, tm, d), lambda b, i, j: (b, i, 0), pipeline_mode=pl.Buffered(1)),
            pl.BlockSpec((1, 6, d), lambda b, i, j: (b, 0, 0)),
            pl.BlockSpec((1, d), lambda b, i, j: (0, 0)),
            pl.BlockSpec((d, tf), lambda b, i, j: (0, j)),
            pl.BlockSpec((d, tf), lambda b, i, j: (0, j)),
            pl.BlockSpec((tf, d), lambda b, i, j: (j, 0)),
        ],
        out_specs=pl.BlockSpec((1, tm, d), lambda b, i, j: (b, i, 0)),
        out_shape=jax.ShapeDtypeStruct((nb, s, d), F32),
        scratch_shapes=[pltpu.VMEM((tm, d), BF16)],
        compiler_params=_cparams(("arbitrary", "arbitrary", "arbitrary")),
        name="dense_ffn",
    )(x, mod, norm_g.reshape(1, d), w_gate, w_up, w_down)


def _attn_kernel(slopes_ref, q_ref, k_ref, v_ref, dg_ref, lq1_ref, lk1_ref, lq2_ref, lk2_ref,
                 sg_ref, o_ref, vt_ref, ta_ref, tb_ref, m_ref, l_ref, acc_ref, *, lambda_init):
    tq = q_ref.shape[1]
    s_len = k_ref.shape[1]
    head = pl.program_id(1)
    qi = pl.program_id(2)
    slope = jnp.full((1, LANES), slopes_ref[head], F32)

    @pl.when(qi == 0)
    def _():
        for c in range(s_len // tq):
            blk = v_ref[0, c * tq:(c + 1) * tq, :].astype(F32)
            vt_ref[:, c * tq:(c + 1) * tq] = blk.T.astype(BF16)

    lam = (jnp.exp(jnp.sum(lq1_ref[...] * lk1_ref[...], axis=-1, keepdims=True))
           - jnp.exp(jnp.sum(lq2_ref[...] * lk2_ref[...], axis=-1, keepdims=True))
           + lambda_init)

    q = q_ref[0]
    lane = lax.broadcasted_iota(jnp.int32, q.shape, 1)
    q1 = jnp.where(lane < HEAD_DIM, q, jnp.zeros_like(q))
    q2 = jnp.where(lane >= HEAD_DIM, q, jnp.zeros_like(q))
    nt = (((1,), (1,)), ((), ()))
    q0 = qi * tq

    s_hi = slope.astype(BF16).astype(F32)
    s_mid = (slope - s_hi).astype(BF16).astype(F32)
    s_lo = ((slope - s_hi) - s_mid).astype(BF16).astype(F32)
    alane = lax.broadcasted_iota(jnp.int32, (2 * tq, LANES), 1)
    piece = jnp.where(alane % 3 == 0, s_hi, jnp.where(alane % 3 == 1, s_mid, s_lo))
    q_aug = jnp.where(alane < 6, piece, 0.0).astype(BF16)
    qq = jnp.concatenate([jnp.concatenate([q1, q2], axis=0), q_aug], axis=1)
    krow = lax.broadcasted_iota(jnp.int32, (tq, LANES), 0)
    klane = lax.broadcasted_iota(jnp.int32, (tq, LANES), 1)
    j_split = jnp.where(klane < 3, krow % BF16_EXACT_INT, (krow // BF16_EXACT_INT) * BF16_EXACT_INT)
    k_aug = jnp.where(klane < 6, j_split, 0).astype(F32).astype(BF16)

    def scores(c, t_ref):
        ks = pl.multiple_of(c * tq, tq)
        kc = jnp.concatenate([k_ref[0, pl.ds(ks, tq), :], k_aug], axis=1)
        t_ref[...] = lax.dot_general(kc, qq, nt, preferred_element_type=F32)

    def absorb(c, t_ref, bias, shift):
        ks = pl.multiple_of(c * tq, tq)
        t = t_ref[...] if bias is None else t_ref[...] + bias
        m = m_ref[...]
        m_new = jnp.maximum(m, jnp.max(t, axis=0, keepdims=True) + shift)
        p = jnp.exp2(t - (m_new - shift))
        alpha = jnp.exp2(m - m_new)
        l_ref[...] = alpha * l_ref[...] + jnp.sum(p, axis=0, keepdims=True)
        acc_ref[...] = alpha * acc_ref[...] + jnp.dot(vt_ref[:, pl.ds(ks, tq)], p.astype(BF16),
                                                      preferred_element_type=F32)
        m_ref[...] = m_new

    def absorb_past(c, t_ref):
        absorb(c, t_ref, None, slope[:, 0:1] * (c * tq - q0).astype(F32))

    def absorb_diag(t_ref):
        diag = slope[:, 0:1] * dg_ref[...]
        absorb(qi, t_ref, jnp.concatenate([diag, diag], axis=1), jnp.zeros((1, 1), F32))

    m_ref[...] = jnp.full(m_ref.shape, -jnp.inf, F32)
    l_ref[...] = jnp.zeros(l_ref.shape, F32)
    acc_ref[...] = jnp.zeros(acc_ref.shape, F32)
    scores(0, ta_ref)

    def pair(p, carry):
        c = 2 * p
        scores(c + 1, tb_ref)
        absorb_past(c, ta_ref)
        scores(c + 2, ta_ref)
        absorb_past(c + 1, tb_ref)
        return carry

    lax.fori_loop(0, qi // 2, pair, 0)
    odd = qi % 2 == 1

    @pl.when(odd)
    def _():
        scores(qi, tb_ref)
        absorb_past(qi - 1, ta_ref)
        absorb_diag(tb_ref)

    @pl.when(jnp.logical_not(odd))
    def _():
        absorb_diag(ta_ref)

    acc = acc_ref[...]
    l = l_ref[...]
    o = acc[:, :tq] / l[:, :tq] - lam * (acc[:, tq:] / l[:, tq:])
    o = o * lax.rsqrt(jnp.mean(o * o, axis=0, keepdims=True) + RMS_EPS)
    o_ref[0] = ((o.T * sg_ref[...]) * (1.0 - lambda_init)).astype(o_ref.dtype)


def _diag_bias(tq):
    j = np.arange(tq)[:, None]
    r = np.arange(tq)[None, :]
    tile = (np.minimum(j, 2 * r - j) - j).astype(np.float32)
    return jnp.asarray(np.where(j // CHUNK <= r // CHUNK, tile, -np.inf).astype(np.float32))


def _diff_attention(qkv, lam_q1, lam_k1, lam_q2, lam_k2, subln_g, lambda_init):
    nb, s, n3 = qkv.shape
    d = n3 // 3
    heads = d // HEAD_LANES
    tq = _tile(s, ATTN_BLOCK)
    ratio = 2.0 ** (-8.0 / heads)
    slopes = np.array([ratio ** (h + 1) for h in range(heads)], dtype=np.float32)
    slopes2 = jnp.asarray(slopes * np.float32(LOG2E))
    vec = lambda a: a.reshape(1, HEAD_DIM)
    small = pl.BlockSpec((1, HEAD_DIM), lambda b, h, i: (0, 0))
    return pl.pallas_call(
        functools.partial(_attn_kernel, lambda_init=lambda_init),
        grid=(nb, heads, s // tq),
        in_specs=[
            pl.BlockSpec(memory_space=pltpu.SMEM),
            pl.BlockSpec((1, tq, HEAD_LANES), lambda b, h, i: (b, i, h)),
            pl.BlockSpec((1, s, HEAD_LANES), lambda b, h, i: (b, 0, heads + h)),
            pl.BlockSpec((1, s, HEAD_LANES), lambda b, h, i: (b, 0, 2 * heads + h)),
            pl.BlockSpec((tq, tq), lambda b, h, i: (0, 0)),
            small, small, small, small,
            pl.BlockSpec((1, HEAD_LANES), lambda b, h, i: (0, 0)),
        ],
        out_specs=pl.BlockSpec((1, tq, HEAD_LANES), lambda b, h, i: (b, i, h)),
        out_shape=jax.ShapeDtypeStruct((nb, s, d), BF16),
        scratch_shapes=[pltpu.VMEM((HEAD_LANES, s), BF16),
                        pltpu.VMEM((tq, 2 * tq), F32), pltpu.VMEM((tq, 2 * tq), F32),
                        pltpu.VMEM((1, 2 * tq), F32), pltpu.VMEM((1, 2 * tq), F32),
                        pltpu.VMEM((HEAD_LANES, 2 * tq), F32)],
        compiler_params=_cparams(("arbitrary", "arbitrary", "arbitrary")),
        name="diff_attention",
    )(slopes2, qkv, qkv, qkv, _diag_bias(tq), vec(lam_q1), vec(lam_k1), vec(lam_q2), vec(lam_k2),
      subln_g.reshape(1, HEAD_LANES))


R_E1, R_E2, R_W1, R_W2, R_RANK1, R_RANK2 = range(6)


def _wo_router_kernel(a_ref, x_ref, mod_ref, g_ref, wo_ref, wr_ref, x_out, h_out, r_out, cnt_out,
                      cnt_ref):
    tm = a_ref.shape[1]

    @pl.when((pl.program_id(0) == 0) & (pl.program_id(1) == 0))
    def _():
        cnt_ref[...] = jnp.zeros_like(cnt_ref)

    mix = jnp.dot(a_ref[0], wo_ref[...], preferred_element_type=F32)
    xn = x_ref[0] + mod_ref[0, 2:3, :] * mix
    x_out[0] = xn
    inv = lax.rsqrt(jnp.mean(xn * xn, axis=-1, keepdims=True) + RMS_EPS)
    h = ((xn * inv) * g_ref[...]) * (1.0 + mod_ref[0, 4:5, :]) + mod_ref[0, 3:4, :]
    h_out[0] = h

    lane = lax.broadcasted_iota(jnp.int32, (tm, LANES), 1).astype(F32)
    logits = jnp.full((tm, LANES), -jnp.inf, F32)
    for e in range(N_EXPERTS):
        col = jnp.sum(h * wr_ref[e:e + 1, :], axis=-1, keepdims=True)
        logits = jnp.where(lane == e, col, logits)
    v1 = jnp.max(logits, axis=-1, keepdims=True)
    e1 = jnp.min(jnp.where(logits == v1, lane, float(LANES)), axis=-1, keepdims=True)
    hot1 = lane == e1
    rest = jnp.where(hot1, -jnp.inf, logits)
    v2 = jnp.max(rest, axis=-1, keepdims=True)
    e2 = jnp.min(jnp.where(rest == v2, lane, float(LANES)), axis=-1, keepdims=True)
    hot2 = lane == e2
    ex = jnp.exp(v2 - v1)
    w1 = 1.0 / (1.0 + ex)
    w2 = ex / (1.0 + ex)

    hot = jnp.where(hot1 | hot2, 1.0, 0.0)
    ri = lax.broadcasted_iota(jnp.int32, (tm, tm), 0)
    ci = lax.broadcasted_iota(jnp.int32, (tm, tm), 1)
    tri = jnp.where(ci < ri, 1.0, 0.0).astype(BF16)
    before = jnp.dot(tri, hot.astype(BF16), preferred_element_type=F32) + cnt_ref[...]
    rank1 = jnp.sum(jnp.where(hot1, before, 0.0), axis=-1, keepdims=True)
    rank2 = jnp.sum(jnp.where(hot2, before, 0.0), axis=-1, keepdims=True)
    cnt_ref[...] += jnp.sum(hot, axis=0, keepdims=True)
    cnt_out[...] = cnt_ref[...]

    rec = jnp.where(lane == R_E1, e1, 0.0)
    rec = jnp.where(lane == R_E2, e2, rec)
    rec = jnp.where(lane == R_W1, w1, rec)
    rec = jnp.where(lane == R_W2, w2, rec)
    rec = jnp.where(lane == R_RANK1, rank1, rec)
    rec = jnp.where(lane == R_RANK2, rank2, rec)
    r_out[0] = rec


def _wo_router(attn, x, mod, norm_g, w_o, w_router):
    nb, s, d = x.shape
    tm = _tile(s, 512)
    wr = w_router.T
    return pl.pallas_call(
        _wo_router_kernel,
        grid=(nb, s // tm),
        in_specs=[
            pl.BlockSpec((1, tm, d), lambda b, i: (b, i, 0)),
            pl.BlockSpec((1, tm, d), lambda b, i: (b, i, 0)),
            pl.BlockSpec((1, 6, d), lambda b, i: (b, 0, 0)),
            pl.BlockSpec((1, d), lambda b, i: (0, 0)),
            pl.BlockSpec((d, d), lambda b, i: (0, 0)),
            pl.BlockSpec((N_EXPERTS, d), lambda b, i: (0, 0)),
        ],
        out_specs=[
            pl.BlockSpec((1, tm, d), lambda b, i: (b, i, 0)),
            pl.BlockSpec((1, tm, d), lambda b, i: (b, i, 0)),
            pl.BlockSpec((1, tm, LANES), lambda b, i: (b, i, 0)),
            pl.BlockSpec((1, LANES), lambda b, i: (0, 0)),
        ],
        out_shape=[
            jax.ShapeDtypeStruct((nb, s, d), F32),
            jax.ShapeDtypeStruct((nb, s, d), F32),
            jax.ShapeDtypeStruct((nb, s, LANES), F32),
            jax.ShapeDtypeStruct((1, LANES), F32),
        ],
        scratch_shapes=[pltpu.VMEM((1, LANES), F32)],
        compiler_params=_cparams(("arbitrary", "arbitrary")),
        name="wo_router",
    )(attn, x, mod, norm_g.reshape(1, d), w_o, wr)


def _dispatch_kernel(pos_ref, h_ref, xs_in, xs_out, sem, *, td):
    del xs_in

    def row_copy(t, p):
        return pltpu.make_async_copy(h_ref.at[pl.ds(t, 1)], xs_out.at[pl.ds(p, 1)], sem)

    def issue(t, carry):
        row_copy(t, pos_ref[2 * t]).start()
        row_copy(t, pos_ref[2 * t + 1]).start()
        return carry

    lax.fori_loop(0, td, issue, 0, unroll=DMA_UNROLL)

    def drain(t, carry):
        row_copy(0, 0).wait()
        row_copy(0, 0).wait()
        return carry

    lax.fori_loop(0, td, drain, 0, unroll=DMA_UNROLL)


def _dispatch(h2, pos, xs_zero):
    n, d = h2.shape
    td = _tile(n, 512)
    return pl.pallas_call(
        functools.partial(_dispatch_kernel, td=td),
        grid=(n // td,),
        in_specs=[
            pl.BlockSpec((2 * td,), lambda i: (i,), memory_space=pltpu.SMEM),
            pl.BlockSpec((td, d), lambda i: (i, 0)),
            pl.BlockSpec(memory_space=pl.ANY),
        ],
        out_specs=pl.BlockSpec(memory_space=pl.ANY),
        out_shape=jax.ShapeDtypeStruct(xs_zero.shape, xs_zero.dtype),
        scratch_shapes=[pltpu.SemaphoreType.DMA(())],
        input_output_aliases={2: 0},
        compiler_params=_cparams(("arbitrary",)),
        name="dispatch",
    )(pos.reshape(-1), h2, xs_zero)


def _moe_kernel(be_ref, bv_ref, xs_ref, wg_ref, wu_ref, wd_ref, o_ref, xb_ref):
    i = pl.program_id(0)
    j = pl.program_id(1)
    valid = bv_ref[i] > 0

    @pl.when(valid & (j == 0))
    def _():
        xb_ref[...] = xs_ref[...].astype(BF16)

    @pl.when(valid)
    def _():
        xb = xb_ref[...]
        gate = jnp.dot(xb, wg_ref[0], preferred_element_type=F32)
        up = jnp.dot(xb, wu_ref[0], preferred_element_type=F32)
        a = ((gate * _sigmoid(gate)) * up).astype(BF16)
        part = jnp.dot(a, wd_ref[0], preferred_element_type=F32)

        @pl.when(j == 0)
        def _():
            o_ref[...] = part

        @pl.when(j > 0)
        def _():
            o_ref[...] += part

    @pl.when(jnp.logical_not(valid) & (j == 0))
    def _():
        o_ref[...] = jnp.zeros_like(o_ref)


def _moe_experts(xs, blk_expert, blk_valid, w_gate, w_up, w_down, tme):
    p, d = xs.shape
    f = w_gate.shape[2]
    tf = _tile(f, 1024)
    nf = f // tf

    def fj(i, j, bv):
        return jnp.where(bv[i] > 0, j, nf - 1)

    grid_spec = pltpu.PrefetchScalarGridSpec(
        num_scalar_prefetch=2,
        grid=(p // tme, nf),
        in_specs=[
            pl.BlockSpec((tme, d), lambda i, j, be, bv: (i, 0)),
            pl.BlockSpec((1, d, tf), lambda i, j, be, bv: (be[i], 0, fj(i, j, bv))),
            pl.BlockSpec((1, d, tf), lambda i, j, be, bv: (be[i], 0, fj(i, j, bv))),
            pl.BlockSpec((1, tf, d), lambda i, j, be, bv: (be[i], fj(i, j, bv), 0)),
        ],
        out_specs=pl.BlockSpec((tme, d), lambda i, j, be, bv: (i, 0)),
        scratch_shapes=[pltpu.VMEM((tme, d), BF16)],
    )
    return pl.pallas_call(
        _moe_kernel,
        grid_spec=grid_spec,
        out_shape=jax.ShapeDtypeStruct((p, d), F32),
        compiler_params=_cparams(("arbitrary", "arbitrary")),
        name="moe_experts",
    )(blk_expert, blk_valid, xs, w_gate, w_up, w_down)


def _final_kernel(pos_ref, ys_hbm, x_ref, r_ref, mod_ref, g_ref, o_ref, buf_ref, sem):
    tm = x_ref.shape[1]

    def row_copy(t, k, p):
        return pltpu.make_async_copy(ys_hbm.at[pl.ds(p, 1)], buf_ref.at[k, pl.ds(t, 1)], sem)

    def issue(t, carry):
        row_copy(t, 0, pos_ref[2 * t]).start()
        row_copy(t, 1, pos_ref[2 * t + 1]).start()
        return carry

    lax.fori_loop(0, tm, issue, 0, unroll=DMA_UNROLL)

    def drain(t, carry):
        row_copy(0, 0, 0).wait()
        row_copy(0, 1, 0).wait()
        return carry

    lax.fori_loop(0, tm, drain, 0, unroll=DMA_UNROLL)

    rec = r_ref[0]
    w1 = rec[:, R_W1:R_W1 + 1]
    w2 = rec[:, R_W2:R_W2 + 1]
    y = w1 * buf_ref[0] + w2 * buf_ref[1]
    xn = x_ref[0] + mod_ref[0, 5:6, :] * y
    inv = lax.rsqrt(jnp.mean(xn * xn, axis=-1, keepdims=True) + RMS_EPS)
    o_ref[0] = (xn * inv) * g_ref[...]


def _combine_final(ys, pos, x, route, mod, final_g):
    nb, s, d = x.shape
    tm = _tile(s, 512)
    ni = s // tm
    return pl.pallas_call(
        _final_kernel,
        grid=(nb, ni),
        in_specs=[
            pl.BlockSpec((2 * tm,), lambda b, i: (b * ni + i,), memory_space=pltpu.SMEM),
            pl.BlockSpec(memory_space=pl.ANY),
            pl.BlockSpec((1, tm, d), lambda b, i: (b, i, 0)),
            pl.BlockSpec((1, tm, LANES), lambda b, i: (b, i, 0)),
            pl.BlockSpec((1, 6, d), lambda b, i: (b, 0, 0)),
            pl.BlockSpec((1, d), lambda b, i: (0, 0)),
        ],
        out_specs=pl.BlockSpec((1, tm, d), lambda b, i: (b, i, 0)),
        out_shape=jax.ShapeDtypeStruct((nb, s, d), F32),
        scratch_shapes=[pltpu.VMEM((2, tm, d), F32), pltpu.SemaphoreType.DMA(())],
        compiler_params=_cparams(("arbitrary", "arbitrary")),
        name="combine_final",
    )(pos.reshape(-1), ys, x, route, mod, final_g.reshape(1, d))


def _lambda_init(layer_idx):
    return 0.8 - 0.6 * math.exp(-0.3 * layer_idx)


def kernel(x, c, mod_w, mod_b, norm1_g, norm2_g, conv_w_in, conv_b_in, conv_w_dw, conv_b_dw, conv_ln_g, conv_ln_b, conv_w_out, conv_b_out, attn_w_qkv, attn_w_o, attn_lam_q1, attn_lam_k1, attn_lam_q2, attn_lam_k2, attn_subln_g, ffn_w_gate, ffn_w_up, ffn_w_down, moe_w_router, moe_w_gate, moe_w_up, moe_w_down, final_g):
    nb, s, d = x.shape
    n = nb * s
    bf = lambda w: w.astype(BF16)

    mod = _modulation(c, mod_w, mod_b).reshape(mod_w.shape[0], nb, 6, d)

    u = _conv_in(x, mod[0], norm1_g[0], bf(conv_w_in[0]), conv_b_in[0])
    x = _conv_out(u, x, mod[0], conv_w_dw[0], conv_b_dw[0], conv_ln_g[0], conv_ln_b[0],
                  bf(conv_w_out[0]), conv_b_out[0])
    x = _dense_ffn(x, mod[0], norm2_g[0], bf(ffn_w_gate[0]), bf(ffn_w_up[0]), bf(ffn_w_down[0]))

    qkv = _qkv_proj(x, mod[1], norm1_g[1], bf(attn_w_qkv[0]))
    attn = _diff_attention(qkv, attn_lam_q1[0], attn_lam_k1[0], attn_lam_q2[0], attn_lam_k2[0],
                           attn_subln_g[0], _lambda_init(1))
    x, h2, route, counts = _wo_router(attn, x, mod[1], norm2_g[1], bf(attn_w_o[0]),
                                      moe_w_router[0])

    tme = _tile(n, 512)
    nblk = 2 * n // tme + N_EXPERTS
    route2 = route.reshape(n, LANES)
    eidx = route2[:, R_E1:R_E2 + 1].astype(jnp.int32)
    rank = route2[:, R_RANK1:R_RANK2 + 1].astype(jnp.int32)
    cnt = counts[0, :N_EXPERTS].astype(jnp.int32)
    blocks = (cnt + tme - 1) // tme
    blk_end = jnp.cumsum(blocks)
    start = (blk_end - blocks) * tme
    pos = start[eidx] + rank
    total = blk_end[-1]
    bid = jnp.arange(nblk, dtype=jnp.int32)
    blk_valid = (bid < total).astype(jnp.int32)
    blk_expert = jnp.searchsorted(blk_end, jnp.minimum(bid, total - 1), side="right")
    blk_expert = jnp.minimum(blk_expert, N_EXPERTS - 1).astype(jnp.int32)

    xs = _dispatch(h2.reshape(n, d), pos, jnp.zeros((nblk * tme, d), F32))
    ys = _moe_experts(xs, blk_expert, blk_valid, bf(moe_w_gate[0]), bf(moe_w_up[0]),
                      bf(moe_w_down[0]), tme)
    return _combine_final(ys, pos, x, route, mod[1], final_g)
```

```python
import functools
import math

import numpy as np
import jax
import jax.numpy as jnp
from jax import lax
from jax.experimental import pallas as pl
from jax.experimental.pallas import tpu as pltpu

F32 = jnp.float32
BF16 = jnp.bfloat16

RMS_EPS = 1e-6
LN_EPS = 1e-5
CHUNK = 64
CONV_WIDTH = 31
HEAD_DIM = 64
HEAD_LANES = 2 * HEAD_DIM
LOG2E = math.log2(math.e)
QUERY_SCALE = HEAD_DIM ** -0.5 * LOG2E
BF16_EXACT_INT = 256
ATTN_BLOCK = 512
N_EXPERTS = 8
LANES = 128
SUBLANES = 8
CONV_LANES = 512
DMA_UNROLL = 8
HALO = 32
VMEM_LIMIT = 56 * 1024 * 1024


def _tile(n, pref):
    t = min(n, pref)
    while n % t:
        t //= 2
    return t


def _sigmoid(x):
    return 1.0 / (1.0 + jnp.exp(-x))


U32 = jnp.uint32
HIGH_HALF = np.uint32(0xFFFF0000)
HALF_BITS = np.uint32(16)


def _pack_rows(x):
    half = x.shape[1] // 2
    bits = lax.bitcast_convert_type(x.astype(BF16).astype(F32), U32)
    return (bits[:, :half] >> HALF_BITS) | (bits[:, half:] & HIGH_HALF)


def _unpack_rows(w):
    lo = lax.bitcast_convert_type(w << HALF_BITS, F32)
    hi = lax.bitcast_convert_type(w & HIGH_HALF, F32)
    return jnp.concatenate([lo, hi], axis=1)


def _cparams(sem):
    return pltpu.CompilerParams(dimension_semantics=sem, vmem_limit_bytes=VMEM_LIMIT)


def _mod_kernel(c_ref, w_ref, b_ref, o_ref):
    c = c_ref[...]
    ca = (c * _sigmoid(c)).astype(BF16)
    o_ref[0] = jnp.dot(ca, w_ref[0].astype(BF16), preferred_element_type=F32) + b_ref[0]


def _modulation(c, mod_w, mod_b):
    depth, d, n6 = mod_w.shape
    nb = c.shape[0]
    tn = _tile(n6, 1024)
    return pl.pallas_call(
        _mod_kernel,
        grid=(depth, n6 // tn),
        in_specs=[
            pl.BlockSpec((nb, d), lambda l, j: (0, 0)),
            pl.BlockSpec((1, d, tn), lambda l, j: (l, 0, j)),
            pl.BlockSpec((1, 1, tn), lambda l, j: (l, 0, j)),
        ],
        out_specs=pl.BlockSpec((1, nb, tn), lambda l, j: (l, 0, j)),
        out_shape=jax.ShapeDtypeStruct((depth, nb, n6), F32),
        compiler_params=_cparams(("arbitrary", "arbitrary")),
        name="modulation",
    )(c, mod_w, mod_b.reshape(depth, 1, n6))


def _norm_mod_rows(x_ref, g, shift, scale, h_ref, rows, rc):
    one_plus = 1.0 + scale

    def body(r, carry):
        rs = pl.multiple_of(r * rc, rc)
        xv = x_ref[0, pl.ds(rs, rc), :]
        inv = lax.rsqrt(jnp.mean(xv * xv, axis=-1, keepdims=True) + RMS_EPS)
        h = ((xv * inv) * g) * one_plus + shift
        h_ref[pl.ds(rs, rc), :] = h.astype(h_ref.dtype)
        return carry

    lax.fori_loop(0, rows // rc, body, 0, unroll=2)


def _conv_in_kernel(x_ref, mod_ref, g_ref, wv_ref, wg_ref, bv_ref, bg_ref, o_ref, h_ref, *, rc):
    @pl.when(pl.program_id(2) == 0)
    def _():
        _norm_mod_rows(x_ref, g_ref[...], mod_ref[0, 0:1, :], mod_ref[0, 1:2, :], h_ref,
                       h_ref.shape[0], rc)

    h = h_ref[...]
    val = jnp.dot(h, wv_ref[...], preferred_element_type=F32) + bv_ref[...]
    gate = jnp.dot(h, wg_ref[...], preferred_element_type=F32) + bg_ref[...]
    o_ref[0] = (val * _sigmoid(gate)).astype(o_ref.dtype)


def _conv_in(x, mod, norm_g, w_in, b_in):
    nb, s, d = x.shape
    tm = _tile(s, 1024)
    tn = _tile(d, 1024)
    nj = d // tn
    b2 = b_in.reshape(1, 2 * d)
    return pl.pallas_call(
        functools.partial(_conv_in_kernel, rc=_tile(tm, 32)),
        grid=(nb, s // tm, nj),
        in_specs=[
            pl.BlockSpec((1, tm, d), lambda b, i, j: (b, i, 0)),
            pl.BlockSpec((1, 6, d), lambda b, i, j: (b, 0, 0)),
            pl.BlockSpec((1, d), lambda b, i, j: (0, 0)),
            pl.BlockSpec((d, tn), lambda b, i, j: (0, j)),
            pl.BlockSpec((d, tn), lambda b, i, j: (0, j + nj)),
            pl.BlockSpec((1, tn), lambda b, i, j: (0, j)),
            pl.BlockSpec((1, tn), lambda b, i, j: (0, j + nj)),
        ],
        out_specs=pl.BlockSpec((1, tm, tn), lambda b, i, j: (b, i, j)),
        out_shape=jax.ShapeDtypeStruct((nb, s, d), BF16),
        scratch_shapes=[pltpu.VMEM((tm, d), BF16)],
        compiler_params=_cparams(("arbitrary", "arbitrary", "arbitrary")),
        name="conv_in_glu",
    )(x, mod, norm_g.reshape(1, d), w_in, w_in, b2, b2)


def _qkv_kernel(x_ref, mod_ref, g_ref, w_ref, o_ref, h_ref, *, rc, nq):
    @pl.when(pl.program_id(2) == 0)
    def _():
        _norm_mod_rows(x_ref, g_ref[...], mod_ref[0, 0:1, :], mod_ref[0, 1:2, :], h_ref,
                       h_ref.shape[0], rc)

    acc = jnp.dot(h_ref[...], w_ref[...], preferred_element_type=F32)
    is_query = pl.program_id(2) < nq

    @pl.when(is_query)
    def _():
        o_ref[0] = (acc * QUERY_SCALE).astype(o_ref.dtype)

    @pl.when(jnp.logical_not(is_query))
    def _():
        o_ref[0] = acc.astype(o_ref.dtype)


def _qkv_proj(x, mod, norm_g, w_qkv):
    nb, s, d = x.shape
    n3 = w_qkv.shape[1]
    tm = _tile(s, 1024)
    tn = _tile(n3 // 3, 1024)
    return pl.pallas_call(
        functools.partial(_qkv_kernel, rc=_tile(tm, 32), nq=n3 // 3 // tn),
        grid=(nb, s // tm, n3 // tn),
        in_specs=[
            pl.BlockSpec((1, tm, d), lambda b, i, j: (b, i, 0)),
            pl.BlockSpec((1, 6, d), lambda b, i, j: (b, 0, 0)),
            pl.BlockSpec((1, d), lambda b, i, j: (0, 0)),
            pl.BlockSpec((d, tn), lambda b, i, j: (0, j)),
        ],
        out_specs=pl.BlockSpec((1, tm, tn), lambda b, i, j: (b, i, j)),
        out_shape=jax.ShapeDtypeStruct((nb, s, n3), BF16),
        scratch_shapes=[pltpu.VMEM((tm, d), BF16)],
        compiler_params=_cparams(("arbitrary", "arbitrary", "arbitrary")),
        name="qkv_proj",
    )(x, mod, norm_g.reshape(1, d), w_qkv)


def _conv_out_kernel(u_ref, halo_ref, x_ref, mod_ref, wdw_ref, bdw_ref, lng_ref, lnb_ref,
                     wo_ref, bo_ref, o_ref, buf_ref, cv_ref, h_ref, *, rc, cr):
    tm = u_ref.shape[1]
    first = pl.program_id(1) == 0
    halo = halo_ref[0].astype(F32)
    buf_ref[0:HALO, :] = jnp.where(first, jnp.zeros_like(halo), halo)

    def fill(r, carry):
        rs = pl.multiple_of(r * rc, rc)
        buf_ref[pl.ds(HALO + rs, rc), :] = u_ref[0, pl.ds(rs, rc), :].astype(F32)
        return carry

    lax.fori_loop(0, tm // rc, fill, 0)

    lng = lng_ref[...]
    lnb = lnb_ref[...]
    lead = HALO - (CONV_WIDTH - 1)
    d = buf_ref.shape[1]
    cw = _tile(d, CONV_LANES)
    win = cr + HALO

    def taps(r, carry):
        rs = pl.multiple_of(r * cr, cr)
        for lb in range(d // cw):
            cols = slice(lb * cw, (lb + 1) * cw)
            w = buf_ref[pl.ds(rs, win), cols]
            acc = jnp.zeros((cr, cw), F32) + bdw_ref[:, cols]
            for sh in range(SUBLANES):
                ws = w if sh == 0 else pltpu.roll(w, win - sh, axis=0)
                for a in range(win // SUBLANES):
                    k = SUBLANES * a + sh - lead
                    if 0 <= k < CONV_WIDTH:
                        acc = acc + wdw_ref[k:k + 1, cols] * ws[SUBLANES * a:SUBLANES * a + cr]
            cv_ref[pl.ds(rs, cr), cols] = acc
        return carry

    lax.fori_loop(0, tm // cr, taps, 0)

    def conv(r, carry):
        rs = pl.multiple_of(r * rc, rc)
        acc = cv_ref[pl.ds(rs, rc), :]
        mu = jnp.mean(acc, axis=-1, keepdims=True)
        cen = acc - mu
        var = jnp.mean(cen * cen, axis=-1, keepdims=True)
        y = (cen * lax.rsqrt(var + LN_EPS)) * lng + lnb
        h_ref[pl.ds(rs, rc), :] = (y * _sigmoid(y)).astype(h_ref.dtype)
        return carry

    lax.fori_loop(0, tm // rc, conv, 0, unroll=2)

    mix = jnp.dot(h_ref[...], wo_ref[...], preferred_element_type=F32) + bo_ref[...]
    o_ref[0] = x_ref[0] + mod_ref[0, 2:3, :] * mix


def _conv_out(u, x, mod, w_dw, b_dw, ln_g, ln_b, w_out, b_out):
    nb, s, d = x.shape
    tm = _tile(s, 512)
    hb = tm // HALO
    row = lambda a: a.reshape(1, d)
    return pl.pallas_call(
        functools.partial(_conv_out_kernel, rc=_tile(tm, 32), cr=_tile(tm, 64)),
        grid=(nb, s // tm),
        in_specs=[
            pl.BlockSpec((1, tm, d), lambda b, i: (b, i, 0)),
            pl.BlockSpec((1, HALO, d), lambda b, i: (b, jnp.maximum(i * hb - 1, 0), 0)),
            pl.BlockSpec((1, tm, d), lambda b, i: (b, i, 0)),
            pl.BlockSpec((1, 6, d), lambda b, i: (b, 0, 0)),
            pl.BlockSpec((CONV_WIDTH, d), lambda b, i: (0, 0)),
            pl.BlockSpec((1, d), lambda b, i: (0, 0)),
            pl.BlockSpec((1, d), lambda b, i: (0, 0)),
            pl.BlockSpec((1, d), lambda b, i: (0, 0)),
            pl.BlockSpec((d, d), lambda b, i: (0, 0)),
            pl.BlockSpec((1, d), lambda b, i: (0, 0)),
        ],
        out_specs=pl.BlockSpec((1, tm, d), lambda b, i: (b, i, 0)),
        out_shape=jax.ShapeDtypeStruct((nb, s, d), F32),
        scratch_shapes=[pltpu.VMEM((HALO + tm, d), F32), pltpu.VMEM((tm, d), F32),
                        pltpu.VMEM((tm, d), BF16)],
        compiler_params=_cparams(("arbitrary", "arbitrary")),
        name="conv_out",
    )(u, u, x, mod, w_dw, row(b_dw), row(ln_g), row(ln_b), w_out, row(b_out))


def _ffn_kernel(x_ref, mod_ref, g_ref, wg_ref, wu_ref, wd_ref, o_ref, h_ref, *, rc):
    j = pl.program_id(2)

    @pl.when(j == 0)
    def _():
        _norm_mod_rows(x_ref, g_ref[...], mod_ref[0, 3:4, :], mod_ref[0, 4:5, :], h_ref,
                       h_ref.shape[0], rc)

    h = h_ref[...]
    gate = jnp.dot(h, wg_ref[...], preferred_element_type=F32)
    up = jnp.dot(h, wu_ref[...], preferred_element_type=F32)
    a = ((gate * _sigmoid(gate)) * up).astype(BF16)
    part = jnp.dot(a, wd_ref[...], preferred_element_type=F32)

    @pl.when(j == 0)
    def _():
        o_ref[0] = part

    @pl.when(j > 0)
    def _():
        o_ref[0] += part

    @pl.when(j == pl.num_programs(2) - 1)
    def _():
        o_ref[0] = x_ref[0] + mod_ref[0, 5:6, :] * o_ref[0]


def _dense_ffn(x, mod, norm_g, w_gate, w_up, w_down):
    nb, s, d = x.shape
    f = w_gate.shape[1]
    tm = _tile(s, 1024)
    tf = _tile(f, 512)
    return pl.pallas_call(
        functools.partial(_ffn_kernel, rc=_tile(tm, 32)),
        grid=(nb, s // tm, f // tf),
        in_specs=[
            pl.BlockSpec((1, tm, d), lambda b, i, j: (b, i, 0), pipeline_mode=pl.Buffered(1)),
            pl.BlockSpec((1, 6, d), lambda b, i, j: (b, 0, 0)),
            pl.BlockSpec((1, d), lambda b, i, j: (0, 0)),
            pl.BlockSpec((d, tf), lambda b, i, j: (0, j)),
            pl.BlockSpec((d, tf), lambda b, i, j: (0, j)),
            pl.BlockSpec((tf, d), lambda b, i, j: (j, 0)),
        ],
        out_specs=pl.BlockSpec((1, tm, d), lambda b, i, j: (b, i, 0)),
        out_shape=jax.ShapeDtypeStruct((nb, s, d), F32),
        scratch_shapes=[pltpu.VMEM((tm, d), BF16)],
        compiler_params=_cparams(("arbitrary", "arbitrary", "arbitrary")),
        name="dense_ffn",
    )(x, mod, norm_g.reshape(1, d), w_gate, w_up, w_down)


def _attn_kernel(slopes_ref, q_ref, k_ref, v_ref, dg_ref, lq1_ref, lk1_ref, lq2_ref, lk2_ref,
                 sg_ref, o_ref, vt_ref, ta_ref, tb_ref, m_ref, l_ref, acc_ref, *, lambda_init):
    tq = q_ref.shape[1]
    s_len = k_ref.shape[1]
    head = pl.program_id(1)
    qi = pl.program_id(2)
    slope = jnp.full((1, LANES), slopes_ref[head], F32)

    @pl.when(qi == 0)
    def _():
        for c in range(s_len // tq):
            blk = v_ref[0, c * tq:(c + 1) * tq, :].astype(F32)
            vt_ref[:, c * tq:(c + 1) * tq] = blk.T.astype(BF16)

    lam = (jnp.exp(jnp.sum(lq1_ref[...] * lk1_ref[...], axis=-1, keepdims=True))
           - jnp.exp(jnp.sum(lq2_ref[...] * lk2_ref[...], axis=-1, keepdims=True))
           + lambda_init)

    q = q_ref[0]
    lane = lax.broadcasted_iota(jnp.int32, q.shape, 1)
    q1 = jnp.where(lane < HEAD_DIM, q, jnp.zeros_like(q))
    q2 = jnp.where(lane >= HEAD_DIM, q, jnp.zeros_like(q))
    nt = (((1,), (1,)), ((), ()))
    q0 = qi * tq

    s_hi = slope.astype(BF16).astype(F32)
    s_mid = (slope - s_hi).astype(BF16).astype(F32)
    s_lo = ((slope - s_hi) - s_mid).astype(BF16).astype(F32)
    alane = lax.broadcasted_iota(jnp.int32, (2 * tq, LANES), 1)
    piece = jnp.where(alane % 3 == 0, s_hi, jnp.where(alane % 3 == 1, s_mid, s_lo))
    q_aug = jnp.where(alane < 6, piece, 0.0).astype(BF16)
    qq = jnp.concatenate([jnp.concatenate([q1, q2], axis=0), q_aug], axis=1)
    krow = lax.broadcasted_iota(jnp.int32, (tq, LANES), 0)
    klane = lax.broadcasted_iota(jnp.int32, (tq, LANES), 1)
    j_split = jnp.where(klane < 3, krow % BF16_EXACT_INT, (krow // BF16_EXACT_INT) * BF16_EXACT_INT)
    k_aug = jnp.where(klane < 6, j_split, 0).astype(F32).astype(BF16)

    def scores(c, t_ref):
        ks = pl.multiple_of(c * tq, tq)
        kc = jnp.concatenate([k_ref[0, pl.ds(ks, tq), :], k_aug], axis=1)
        t_ref[...] = lax.dot_general(kc, qq, nt, preferred_element_type=F32)

    def absorb(c, t_ref, bias, shift):
        ks = pl.multiple_of(c * tq, tq)
        t = t_ref[...] if bias is None else t_ref[...] + bias
        m = m_ref[...]
        m_new = jnp.maximum(m, jnp.max(t, axis=0, keepdims=True) + shift)
        p = jnp.exp2(t - (m_new - shift))
        alpha = jnp.exp2(m - m_new)
        l_ref[...] = alpha * l_ref[...] + jnp.sum(p, axis=0, keepdims=True)
        acc_ref[...] = alpha * acc_ref[...] + jnp.dot(vt_ref[:, pl.ds(ks, tq)], p.astype(BF16),
                                                      preferred_element_type=F32)
        m_ref[...] = m_new

    def absorb_past(c, t_ref):
        absorb(c, t_ref, None, slope[:, 0:1] * (c * tq - q0).astype(F32))

    def absorb_diag(t_ref):
        diag = slope[:, 0:1] * dg_ref[...]
        absorb(qi, t_ref, jnp.concatenate([diag, diag], axis=1), jnp.zeros((1, 1), F32))

    m_ref[...] = jnp.full(m_ref.shape, -jnp.inf, F32)
    l_ref[...] = jnp.zeros(l_ref.shape, F32)
    acc_ref[...] = jnp.zeros(acc_ref.shape, F32)
    scores(0, ta_ref)

    def pair(p, carry):
        c = 2 * p
        scores(c + 1, tb_ref)
        absorb_past(c, ta_ref)
        scores(c + 2, ta_ref)
        absorb_past(c + 1, tb_ref)
        return carry

    lax.fori_loop(0, qi // 2, pair, 0)
    odd = qi % 2 == 1

    @pl.when(odd)
    def _():
        scores(qi, tb_ref)
        absorb_past(qi - 1, ta_ref)
        absorb_diag(tb_ref)

    @pl.when(jnp.logical_not(odd))
    def _():
        absorb_diag(ta_ref)

    acc = acc_ref[...]
    l = l_ref[...]
    o = acc[:, :tq] / l[:, :tq] - lam * (acc[:, tq:] / l[:, tq:])
    o = o * lax.rsqrt(jnp.mean(o * o, axis=0, keepdims=True) + RMS_EPS)
    o_ref[0] = ((o.T * sg_ref[...]) * (1.0 - lambda_init)).astype(o_ref.dtype)


def _diag_bias(tq):
    j = np.arange(tq)[:, None]
    r = np.arange(tq)[None, :]
    tile = (np.minimum(j, 2 * r - j) - j).astype(np.float32)
    return jnp.asarray(np.where(j // CHUNK <= r // CHUNK, tile, -np.inf).astype(np.float32))


def _diff_attention(qkv, lam_q1, lam_k1, lam_q2, lam_k2, subln_g, lambda_init):
    nb, s, n3 = qkv.shape
    d = n3 // 3
    heads = d // HEAD_LANES
    tq = _tile(s, ATTN_BLOCK)
    ratio = 2.0 ** (-8.0 / heads)
    slopes = np.array([ratio ** (h + 1) for h in range(heads)], dtype=np.float32)
    slopes2 = jnp.asarray(slopes * np.float32(LOG2E))
    vec = lambda a: a.reshape(1, HEAD_DIM)
    small = pl.BlockSpec((1, HEAD_DIM), lambda b, h, i: (0, 0))
    return pl.pallas_call(
        functools.partial(_attn_kernel, lambda_init=lambda_init),
        grid=(nb, heads, s // tq),
        in_specs=[
            pl.BlockSpec(memory_space=pltpu.SMEM),
            pl.BlockSpec((1, tq, HEAD_LANES), lambda b, h, i: (b, i, h)),
            pl.BlockSpec((1, s, HEAD_LANES), lambda b, h, i: (b, 0, heads + h)),
            pl.BlockSpec((1, s, HEAD_LANES), lambda b, h, i: (b, 0, 2 * heads + h)),
            pl.BlockSpec((tq, tq), lambda b, h, i: (0, 0)),
            small, small, small, small,
            pl.BlockSpec((1, HEAD_LANES), lambda b, h, i: (0, 0)),
        ],
        out_specs=pl.BlockSpec((1, tq, HEAD_LANES), lambda b, h, i: (b, i, h)),
        out_shape=jax.ShapeDtypeStruct((nb, s, d), BF16),
        scratch_shapes=[pltpu.VMEM((HEAD_LANES, s), BF16),
                        pltpu.VMEM((tq, 2 * tq), F32), pltpu.VMEM((tq, 2 * tq), F32),
                        pltpu.VMEM((1, 2 * tq), F32), pltpu.VMEM((1, 2 * tq), F32),
                        pltpu.VMEM((HEAD_LANES, 2 * tq), F32)],
        compiler_params=_cparams(("arbitrary", "arbitrary", "arbitrary")),
        name="diff_attention",
    )(slopes2, qkv, qkv, qkv, _diag_bias(tq), vec(lam_q1), vec(lam_k1), vec(lam_q2), vec(lam_k2),
      subln_g.reshape(1, HEAD_LANES))


R_E1, R_E2, R_W1, R_W2, R_RANK1, R_RANK2 = range(6)


def _wo_router_kernel(a_ref, x_ref, mod_ref, g_ref, wo_ref, wr_ref, x_out, h_out, r_out, cnt_out,
                      cnt_ref):
    tm = a_ref.shape[1]

    @pl.when((pl.program_id(0) == 0) & (pl.program_id(1) == 0))
    def _():
        cnt_ref[...] = jnp.zeros_like(cnt_ref)

    mix = jnp.dot(a_ref[0], wo_ref[...], preferred_element_type=F32)
    xn = x_ref[0] + mod_ref[0, 2:3, :] * mix
    x_out[0] = xn
    inv = lax.rsqrt(jnp.mean(xn * xn, axis=-1, keepdims=True) + RMS_EPS)
    h = ((xn * inv) * g_ref[...]) * (1.0 + mod_ref[0, 4:5, :]) + mod_ref[0, 3:4, :]
    h_out[0] = _pack_rows(h)

    lane = lax.broadcasted_iota(jnp.int32, (tm, LANES), 1).astype(F32)
    logits = jnp.full((tm, LANES), -jnp.inf, F32)
    for e in range(N_EXPERTS):
        col = jnp.sum(h * wr_ref[e:e + 1, :], axis=-1, keepdims=True)
        logits = jnp.where(lane == e, col, logits)
    v1 = jnp.max(logits, axis=-1, keepdims=True)
    e1 = jnp.min(jnp.where(logits == v1, lane, float(LANES)), axis=-1, keepdims=True)
    hot1 = lane == e1
    rest = jnp.where(hot1, -jnp.inf, logits)
    v2 = jnp.max(rest, axis=-1, keepdims=True)
    e2 = jnp.min(jnp.where(rest == v2, lane, float(LANES)), axis=-1, keepdims=True)
    hot2 = lane == e2
    ex = jnp.exp(v2 - v1)
    w1 = 1.0 / (1.0 + ex)
    w2 = ex / (1.0 + ex)

    hot = jnp.where(hot1 | hot2, 1.0, 0.0)
    ri = lax.broadcasted_iota(jnp.int32, (tm, tm), 0)
    ci = lax.broadcasted_iota(jnp.int32, (tm, tm), 1)
    tri = jnp.where(ci < ri, 1.0, 0.0).astype(BF16)
    before = jnp.dot(tri, hot.astype(BF16), preferred_element_type=F32) + cnt_ref[...]
    rank1 = jnp.sum(jnp.where(hot1, before, 0.0), axis=-1, keepdims=True)
    rank2 = jnp.sum(jnp.where(hot2, before, 0.0), axis=-1, keepdims=True)
    cnt_ref[...] += jnp.sum(hot, axis=0, keepdims=True)
    cnt_out[...] = cnt_ref[...]

    rec = jnp.where(lane == R_E1, e1, 0.0)
    rec = jnp.where(lane == R_E2, e2, rec)
    rec = jnp.where(lane == R_W1, w1, rec)
    rec = jnp.where(lane == R_W2, w2, rec)
    rec = jnp.where(lane == R_RANK1, rank1, rec)
    rec = jnp.where(lane == R_RANK2, rank2, rec)
    r_out[0] = rec


def _wo_router(attn, x, mod, norm_g, w_o, w_router):
    nb, s, d = x.shape
    tm = _tile(s, 512)
    wr = w_router.T
    return pl.pallas_call(
        _wo_router_kernel,
        grid=(nb, s // tm),
        in_specs=[
            pl.BlockSpec((1, tm, d), lambda b, i: (b, i, 0)),
            pl.BlockSpec((1, tm, d), lambda b, i: (b, i, 0)),
            pl.BlockSpec((1, 6, d), lambda b, i: (b, 0, 0)),
            pl.BlockSpec((1, d), lambda b, i: (0, 0)),
            pl.BlockSpec((d, d), lambda b, i: (0, 0)),
            pl.BlockSpec((N_EXPERTS, d), lambda b, i: (0, 0)),
        ],
        out_specs=[
            pl.BlockSpec((1, tm, d), lambda b, i: (b, i, 0)),
            pl.BlockSpec((1, tm, d // 2), lambda b, i: (b, i, 0)),
            pl.BlockSpec((1, tm, LANES), lambda b, i: (b, i, 0)),
            pl.BlockSpec((1, LANES), lambda b, i: (0, 0)),
        ],
        out_shape=[
            jax.ShapeDtypeStruct((nb, s, d), F32),
            jax.ShapeDtypeStruct((nb, s, d // 2), U32),
            jax.ShapeDtypeStruct((nb, s, LANES), F32),
            jax.ShapeDtypeStruct((1, LANES), F32),
        ],
        scratch_shapes=[pltpu.VMEM((1, LANES), F32)],
        compiler_params=_cparams(("arbitrary", "arbitrary")),
        name="wo_router",
    )(attn, x, mod, norm_g.reshape(1, d), w_o, wr)


def _dispatch_kernel(pos_ref, h_ref, xs_in, xs_out, sem, *, td):
    del xs_in

    def row_copy(t, p):
        return pltpu.make_async_copy(h_ref.at[pl.ds(t, 1)], xs_out.at[pl.ds(p, 1)], sem)

    def issue(t, carry):
        row_copy(t, pos_ref[2 * t]).start()
        row_copy(t, pos_ref[2 * t + 1]).start()
        return carry

    lax.fori_loop(0, td, issue, 0, unroll=DMA_UNROLL)

    def drain(t, carry):
        row_copy(0, 0).wait()
        row_copy(0, 0).wait()
        return carry

    lax.fori_loop(0, td, drain, 0, unroll=DMA_UNROLL)


def _dispatch(h2, pos, xs_zero):
    n, d = h2.shape
    td = _tile(n, 512)
    return pl.pallas_call(
        functools.partial(_dispatch_kernel, td=td),
        grid=(n // td,),
        in_specs=[
            pl.BlockSpec((2 * td,), lambda i: (i,), memory_space=pltpu.SMEM),
            pl.BlockSpec((td, d), lambda i: (i, 0)),
            pl.BlockSpec(memory_space=pl.ANY),
        ],
        out_specs=pl.BlockSpec(memory_space=pl.ANY),
        out_shape=jax.ShapeDtypeStruct(xs_zero.shape, xs_zero.dtype),
        scratch_shapes=[pltpu.SemaphoreType.DMA(())],
        input_output_aliases={2: 0},
        compiler_params=_cparams(("arbitrary",)),
        name="dispatch",
    )(pos.reshape(-1), h2, xs_zero)


def _moe_kernel(be_ref, bv_ref, xs_ref, wg_ref, wu_ref, wd_ref, o_ref, xb_ref, acc_ref):
    i = pl.program_id(0)
    j = pl.program_id(1)
    valid = bv_ref[i] > 0

    @pl.when(valid & (j == 0))
    def _():
        xb_ref[...] = _unpack_rows(xs_ref[...]).astype(BF16)

    @pl.when(valid)
    def _():
        xb = xb_ref[...]
        gate = jnp.dot(xb, wg_ref[0], preferred_element_type=F32)
        up = jnp.dot(xb, wu_ref[0], preferred_element_type=F32)
        a = ((gate * _sigmoid(gate)) * up).astype(BF16)
        part = jnp.dot(a, wd_ref[0], preferred_element_type=F32)

        @pl.when(j == 0)
        def _():
            acc_ref[...] = part

        @pl.when(j > 0)
        def _():
            acc_ref[...] += part

        @pl.when(j == pl.num_programs(1) - 1)
        def _():
            o_ref[...] = _pack_rows(acc_ref[...])

    @pl.when(jnp.logical_not(valid) & (j == 0))
    def _():
        o_ref[...] = jnp.zeros_like(o_ref)


def _moe_experts(xs, blk_expert, blk_valid, w_gate, w_up, w_down, tme):
    p, dh = xs.shape
    d = 2 * dh
    f = w_gate.shape[2]
    tf = _tile(f, 1024)
    nf = f // tf

    def fj(i, j, bv):
        return jnp.where(bv[i] > 0, j, nf - 1)

    grid_spec = pltpu.PrefetchScalarGridSpec(
        num_scalar_prefetch=2,
        grid=(p // tme, nf),
        in_specs=[
            pl.BlockSpec((tme, dh), lambda i, j, be, bv: (i, 0)),
            pl.BlockSpec((1, d, tf), lambda i, j, be, bv: (be[i], 0, fj(i, j, bv))),
            pl.BlockSpec((1, d, tf), lambda i, j, be, bv: (be[i], 0, fj(i, j, bv))),
            pl.BlockSpec((1, tf, d), lambda i, j, be, bv: (be[i], fj(i, j, bv), 0)),
        ],
        out_specs=pl.BlockSpec((tme, dh), lambda i, j, be, bv: (i, 0)),
        scratch_shapes=[pltpu.VMEM((tme, d), BF16), pltpu.VMEM((tme, d), F32)],
    )
    return pl.pallas_call(
        _moe_kernel,
        grid_spec=grid_spec,
        out_shape=jax.ShapeDtypeStruct((p, dh), U32),
        compiler_params=_cparams(("arbitrary", "arbitrary")),
        name="moe_experts",
    )(blk_expert, blk_valid, xs, w_gate, w_up, w_down)


def _final_kernel(pos_ref, ys_hbm, x_ref, r_ref, mod_ref, g_ref, o_ref, buf_ref, sem):
    tm = x_ref.shape[1]

    def row_copy(t, k, p):
        return pltpu.make_async_copy(ys_hbm.at[pl.ds(p, 1)], buf_ref.at[k, pl.ds(t, 1)], sem)

    def issue(t, carry):
        row_copy(t, 0, pos_ref[2 * t]).start()
        row_copy(t, 1, pos_ref[2 * t + 1]).start()
        return carry

    lax.fori_loop(0, tm, issue, 0, unroll=DMA_UNROLL)

    def drain(t, carry):
        row_copy(0, 0, 0).wait()
        row_copy(0, 1, 0).wait()
        return carry

    lax.fori_loop(0, tm, drain, 0, unroll=DMA_UNROLL)

    rec = r_ref[0]
    w1 = rec[:, R_W1:R_W1 + 1]
    w2 = rec[:, R_W2:R_W2 + 1]
    y = w1 * _unpack_rows(buf_ref[0]) + w2 * _unpack_rows(buf_ref[1])
    xn = x_ref[0] + mod_ref[0, 5:6, :] * y
    inv = lax.rsqrt(jnp.mean(xn * xn, axis=-1, keepdims=True) + RMS_EPS)
    o_ref[0] = (xn * inv) * g_ref[...]


def _combine_final(ys, pos, x, route, mod, final_g):
    nb, s, d = x.shape
    tm = _tile(s, 512)
    ni = s // tm
    return pl.pallas_call(
        _final_kernel,
        grid=(nb, ni),
        in_specs=[
            pl.BlockSpec((2 * tm,), lambda b, i: (b * ni + i,), memory_space=pltpu.SMEM),
            pl.BlockSpec(memory_space=pl.ANY),
            pl.BlockSpec((1, tm, d), lambda b, i: (b, i, 0)),
            pl.BlockSpec((1, tm, LANES), lambda b, i: (b, i, 0)),
            pl.BlockSpec((1, 6, d), lambda b, i: (b, 0, 0)),
            pl.BlockSpec((1, d), lambda b, i: (0, 0)),
        ],
        out_specs=pl.BlockSpec((1, tm, d), lambda b, i: (b, i, 0)),
        out_shape=jax.ShapeDtypeStruct((nb, s, d), F32),
        scratch_shapes=[pltpu.VMEM((2, tm, d // 2), U32), pltpu.SemaphoreType.DMA(())],
        compiler_params=_cparams(("arbitrary", "arbitrary")),
        name="combine_final",
    )(pos.reshape(-1), ys, x, route, mod, final_g.reshape(1, d))


def _lambda_init(layer_idx):
    return 0.8 - 0.6 * math.exp(-0.3 * layer_idx)


def kernel(x, c, mod_w, mod_b, norm1_g, norm2_g, conv_w_in, conv_b_in, conv_w_dw, conv_b_dw, conv_ln_g, conv_ln_b, conv_w_out, conv_b_out, attn_w_qkv, attn_w_o, attn_lam_q1, attn_lam_k1, attn_lam_q2, attn_lam_k2, attn_subln_g, ffn_w_gate, ffn_w_up, ffn_w_down, moe_w_router, moe_w_gate, moe_w_up, moe_w_down, final_g):
    nb, s, d = x.shape
    n = nb * s
    bf = lambda w: w.astype(BF16)

    mod = _modulation(c, mod_w, mod_b).reshape(mod_w.shape[0], nb, 6, d)

    u = _conv_in(x, mod[0], norm1_g[0], bf(conv_w_in[0]), conv_b_in[0])
    x = _conv_out(u, x, mod[0], conv_w_dw[0], conv_b_dw[0], conv_ln_g[0], conv_ln_b[0],
                  bf(conv_w_out[0]), conv_b_out[0])
    x = _dense_ffn(x, mod[0], norm2_g[0], bf(ffn_w_gate[0]), bf(ffn_w_up[0]), bf(ffn_w_down[0]))

    qkv = _qkv_proj(x, mod[1], norm1_g[1], bf(attn_w_qkv[0]))
    attn = _diff_attention(qkv, attn_lam_q1[0], attn_lam_k1[0], attn_lam_q2[0], attn_lam_k2[0],
                           attn_subln_g[0], _lambda_init(1))
    x, h2, route, counts = _wo_router(attn, x, mod[1], norm2_g[1], bf(attn_w_o[0]),
                                      moe_w_router[0])

    tme = _tile(n, 512)
    nblk = 2 * n // tme + N_EXPERTS
    route2 = route.reshape(n, LANES)
    eidx = route2[:, R_E1:R_E2 + 1].astype(jnp.int32)
    rank = route2[:, R_RANK1:R_RANK2 + 1].astype(jnp.int32)
    cnt = counts[0, :N_EXPERTS].astype(jnp.int32)
    blocks = (cnt + tme - 1) // tme
    blk_end = jnp.cumsum(blocks)
    start = (blk_end - blocks) * tme
    pos = start[eidx] + rank
    total = blk_end[-1]
    bid = jnp.arange(nblk, dtype=jnp.int32)
    blk_valid = (bid < total).astype(jnp.int32)
    blk_expert = jnp.searchsorted(blk_end, jnp.minimum(bid, total - 1), side="right")
    blk_expert = jnp.minimum(blk_expert, N_EXPERTS - 1).astype(jnp.int32)

    xs = _dispatch(h2.reshape(n, d // 2), pos, jnp.zeros((nblk * tme, d // 2), U32))
    ys = _moe_experts(xs, blk_expert, blk_valid, bf(moe_w_gate[0]), bf(moe_w_up[0]),
                      bf(moe_w_down[0]), tme)
    return _combine_final(ys, pos, x, route, mod[1], final_g)
```

```python
import functools
import math

import numpy as np
import jax
import jax.numpy as jnp
from jax import lax
from jax.experimental import pallas as pl
from jax.experimental.pallas import tpu as pltpu

F32 = jnp.float32
BF16 = jnp.bfloat16

RMS_EPS = 1e-6
LN_EPS = 1e-5
CHUNK = 64
CONV_WIDTH = 31
HEAD_DIM = 64
HEAD_LANES = 2 * HEAD_DIM
LOG2E = math.log2(math.e)
QUERY_SCALE = HEAD_DIM ** -0.5 * LOG2E
BF16_EXACT_INT = 256
ATTN_BLOCK = 512
N_EXPERTS = 8
LANES = 128
SUBLANES = 8
CONV_LANES = 512
DMA_UNROLL = 8
HALO = 32
VMEM_LIMIT = 56 * 1024 * 1024


def _tile(n, pref):
    t = min(n, pref)
    while n % t:
        t //= 2
    return t


def _sigmoid(x):
    return 1.0 / (1.0 + jnp.exp(-x))


def _cparams(sem):
    return pltpu.CompilerParams(dimension_semantics=sem, vmem_limit_bytes=VMEM_LIMIT)


def _mod_kernel(c_ref, w_ref, b_ref, o_ref):
    c = c_ref[...]
    ca = (c * _sigmoid(c)).astype(BF16)
    o_ref[0] = jnp.dot(ca, w_ref[0].astype(BF16), preferred_element_type=F32) + b_ref[0]


def _modulation(c, mod_w, mod_b):
    depth, d, n6 = mod_w.shape
    nb = c.shape[0]
    tn = _tile(n6, 1024)
    return pl.pallas_call(
        _mod_kernel,
        grid=(depth, n6 // tn),
        in_specs=[
            pl.BlockSpec((nb, d), lambda l, j: (0, 0)),
            pl.BlockSpec((1, d, tn), lambda l, j: (l, 0, j)),
            pl.BlockSpec((1, 1, tn), lambda l, j: (l, 0, j)),
        ],
        out_specs=pl.BlockSpec((1, nb, tn), lambda l, j: (l, 0, j)),
        out_shape=jax.ShapeDtypeStruct((depth, nb, n6), F32),
        compiler_params=_cparams(("arbitrary", "arbitrary")),
        name="modulation",
    )(c, mod_w, mod_b.reshape(depth, 1, n6))


def _norm_mod_rows(x_ref, g, shift, scale, h_ref, rows, rc):
    one_plus = 1.0 + scale

    def body(r, carry):
        rs = pl.multiple_of(r * rc, rc)
        xv = x_ref[0, pl.ds(rs, rc), :]
        inv = lax.rsqrt(jnp.mean(xv * xv, axis=-1, keepdims=True) + RMS_EPS)
        h = ((xv * inv) * g) * one_plus + shift
        h_ref[pl.ds(rs, rc), :] = h.astype(h_ref.dtype)
        return carry

    lax.fori_loop(0, rows // rc, body, 0, unroll=2)


def _conv_in_kernel(x_ref, mod_ref, g_ref, wv_ref, wg_ref, bv_ref, bg_ref, o_ref, h_ref, *, rc):
    @pl.when(pl.program_id(2) == 0)
    def _():
        _norm_mod_rows(x_ref, g_ref[...], mod_ref[0, 0:1, :], mod_ref[0, 1:2, :], h_ref,
                       h_ref.shape[0], rc)

    h = h_ref[...]
    val = jnp.dot(h, wv_ref[...], preferred_element_type=F32) + bv_ref[...]
    gate = jnp.dot(h, wg_ref[...], preferred_element_type=F32) + bg_ref[...]
    o_ref[0] = (val * _sigmoid(gate)).astype(o_ref.dtype)


def _conv_in(x, mod, norm_g, w_in, b_in):
    nb, s, d = x.shape
    tm = _tile(s, 1024)
    tn = _tile(d, 1024)
    nj = d // tn
    b2 = b_in.reshape(1, 2 * d)
    return pl.pallas_call(
        functools.partial(_conv_in_kernel, rc=_tile(tm, 32)),
        grid=(nb, s // tm, nj),
        in_specs=[
            pl.BlockSpec((1, tm, d), lambda b, i, j: (b, i, 0)),
            pl.BlockSpec((1, 6, d), lambda b, i, j: (b, 0, 0)),
            pl.BlockSpec((1, d), lambda b, i, j: (0, 0)),
            pl.BlockSpec((d, tn), lambda b, i, j: (0, j)),
            pl.BlockSpec((d, tn), lambda b, i, j: (0, j + nj)),
            pl.BlockSpec((1, tn), lambda b, i, j: (0, j)),
            pl.BlockSpec((1, tn), lambda b, i, j: (0, j + nj)),
        ],
        out_specs=pl.BlockSpec((1, tm, tn), lambda b, i, j: (b, i, j)),
        out_shape=jax.ShapeDtypeStruct((nb, s, d), BF16),
        scratch_shapes=[pltpu.VMEM((tm, d), BF16)],
        compiler_params=_cparams(("arbitrary", "arbitrary", "arbitrary")),
        name="conv_in_glu",
    )(x, mod, norm_g.reshape(1, d), w_in, w_in, b2, b2)


def _qkv_kernel(x_ref, mod_ref, g_ref, w_ref, o_ref, h_ref, *, rc, nq):
    @pl.when(pl.program_id(2) == 0)
    def _():
        _norm_mod_rows(x_ref, g_ref[...], mod_ref[0, 0:1, :], mod_ref[0, 1:2, :], h_ref,
                       h_ref.shape[0], rc)

    acc = jnp.dot(h_ref[...], w_ref[...], preferred_element_type=F32)
    is_query = pl.program_id(2) < nq

    @pl.when(is_query)
    def _():
        o_ref[0] = (acc * QUERY_SCALE).astype(o_ref.dtype)

    @pl.when(jnp.logical_not(is_query))
    def _():
        o_ref[0] = acc.astype(o_ref.dtype)


def _qkv_proj(x, mod, norm_g, w_qkv):
    nb, s, d = x.shape
    n3 = w_qkv.shape[1]
    tm = _tile(s, 1024)
    tn = _tile(n3 // 3, 1024)
    return pl.pallas_call(
        functools.partial(_qkv_kernel, rc=_tile(tm, 32), nq=n3 // 3 // tn),
        grid=(nb, s // tm, n3 // tn),
        in_specs=[
            pl.BlockSpec((1, tm, d), lambda b, i, j: (b, i, 0)),
            pl.BlockSpec((1, 6, d), lambda b, i, j: (b, 0, 0)),
            pl.BlockSpec((1, d), lambda b, i, j: (0, 0)),
            pl.BlockSpec((d, tn), lambda b, i, j: (0, j)),
        ],
        out_specs=pl.BlockSpec((1, tm, tn), lambda b, i, j: (b, i, j)),
        out_shape=jax.ShapeDtypeStruct((nb, s, n3), BF16),
        scratch_shapes=[pltpu.VMEM((tm, d), BF16)],
        compiler_params=_cparams(("arbitrary", "arbitrary", "arbitrary")),
        name="qkv_proj",
    )(x, mod, norm_g.reshape(1, d), w_qkv)


def _conv_out_kernel(u_ref, halo_ref, x_ref, mod_ref, wdw_ref, bdw_ref, lng_ref, lnb_ref,
                     wo_ref, bo_ref, o_ref, buf_ref, cv_ref, h_ref, *, rc, cr):
    tm = u_ref.shape[1]
    first = pl.program_id(1) == 0
    halo = halo_ref[0].astype(F32)
    buf_ref[0:HALO, :] = jnp.where(first, jnp.zeros_like(halo), halo)

    def fill(r, carry):
        rs = pl.multiple_of(r * rc, rc)
        buf_ref[pl.ds(HALO + rs, rc), :] = u_ref[0, pl.ds(rs, rc), :].astype(F32)
        return carry

    lax.fori_loop(0, tm // rc, fill, 0)

    lng = lng_ref[...]
    lnb = lnb_ref[...]
    lead = HALO - (CONV_WIDTH - 1)
    d = buf_ref.shape[1]
    cw = _tile(d, CONV_LANES)
    win = cr + HALO

    def taps(r, carry):
        rs = pl.multiple_of(r * cr, cr)
        for lb in range(d // cw):
            cols = slice(lb * cw, (lb + 1) * cw)
            w = buf_ref[pl.ds(rs, win), cols]
            acc = jnp.zeros((cr, cw), F32) + bdw_ref[:, cols]
            for sh in range(SUBLANES):
                ws = w if sh == 0 else pltpu.roll(w, win - sh, axis=0)
                for a in range(win // SUBLANES):
                    k = SUBLANES * a + sh - lead
                    if 0 <= k < CONV_WIDTH:
                        acc = acc + wdw_ref[k:k + 1, cols] * ws[SUBLANES * a:SUBLANES * a + cr]
            cv_ref[pl.ds(rs, cr), cols] = acc
        return carry

    lax.fori_loop(0, tm // cr, taps, 0)

    def conv(r, carry):
        rs = pl.multiple_of(r * rc, rc)
        acc = cv_ref[pl.ds(rs, rc), :]
        mu = jnp.mean(acc, axis=-1, keepdims=True)
        cen = acc - mu
        var = jnp.mean(cen * cen, axis=-1, keepdims=True)
        y = (cen * lax.rsqrt(var + LN_EPS)) * lng + lnb
        h_ref[pl.ds(rs, rc), :] = (y * _sigmoid(y)).astype(h_ref.dtype)
        return carry

    lax.fori_loop(0, tm // rc, conv, 0, unroll=2)

    mix = jnp.dot(h_ref[...], wo_ref[...], preferred_element_type=F32) + bo_ref[...]
    o_ref[0] = x_ref[0] + mod_ref[0, 2:3, :] * mix


def _conv_out(u, x, mod, w_dw, b_dw, ln_g, ln_b, w_out, b_out):
    nb, s, d = x.shape
    tm = _tile(s, 512)
    hb = tm // HALO
    row = lambda a: a.reshape(1, d)
    return pl.pallas_call(
        functools.partial(_conv_out_kernel, rc=_tile(tm, 32), cr=_tile(tm, 64)),
        grid=(nb, s // tm),
        in_specs=[
            pl.BlockSpec((1, tm, d), lambda b, i: (b, i, 0)),
            pl.BlockSpec((1, HALO, d), lambda b, i: (b, jnp.maximum(i * hb - 1, 0), 0)),
            pl.BlockSpec((1, tm, d), lambda b, i: (b, i, 0)),
            pl.BlockSpec((1, 6, d), lambda b, i: (b, 0, 0)),
            pl.BlockSpec((CONV_WIDTH, d), lambda b, i: (0, 0)),
            pl.BlockSpec((1, d), lambda b, i: (0, 0)),
            pl.BlockSpec((1, d), lambda b, i: (0, 0)),
            pl.BlockSpec((1, d), lambda b, i: (0, 0)),
            pl.BlockSpec((d, d), lambda b, i: (0, 0)),
            pl.BlockSpec((1, d), lambda b, i: (0, 0)),
        ],
        out_specs=pl.BlockSpec((1, tm, d), lambda b, i: (b, i, 0)),
        out_shape=jax.ShapeDtypeStruct((nb, s, d), F32),
        scratch_shapes=[pltpu.VMEM((HALO + tm, d), F32), pltpu.VMEM((tm, d), F32),
                        pltpu.VMEM((tm, d), BF16)],
        compiler_params=_cparams(("arbitrary", "arbitrary")),
        name="conv_out",
    )(u, u, x, mod, w_dw, row(b_dw), row(ln_g), row(ln_b), w_out, row(b_out))


def _ffn_kernel(x_ref, mod_ref, g_ref, wg_ref, wu_ref, wd_ref, o_ref, h_ref, *, rc):
    j = pl.program_id(2)

    @pl.when(j == 0)
    def _():
        _norm_mod_rows(x_ref, g_ref[...], mod_ref[0, 3:4, :], mod_ref[0, 4:5, :], h_ref,
                       h_ref.shape[0], rc)

    h = h_ref[...]
    gate = jnp.dot(h, wg_ref[...], preferred_element_type=F32)
    up = jnp.dot(h, wu_ref[...], preferred_element_type=F32)
    a = ((gate * _sigmoid(gate)) * up).astype(BF16)
    part = jnp.dot(a, wd_ref[...], preferred_element_type=F32)

    @pl.when(j == 0)
    def _():
        o_ref[0] = part

    @pl.when(j > 0)
    def _():
        o_ref[0] += part

    @pl.when(j == pl.num_programs(2) - 1)
    def _():
        o_ref[0] = x_ref[0] + mod_ref[0, 5:6, :] * o_ref[0]


def _dense_ffn(x, mod, norm_g, w_gate, w_up, w_down):
    nb, s, d = x.shape
    f = w_gate.shape[1]
    tm = _tile(s, 1024)
    tf = _tile(f, 512)
    return pl.pallas_call(
        functools.partial(_ffn_kernel, rc=_tile(tm, 32)),
        grid=(nb, s // tm, f // tf),
        in_specs=[
            pl.BlockSpec((1, tm, d), lambda b, i, j: (b, i, 0), pipeline_mode=pl.Buffered(1)),
            pl.BlockSpec((1, 6, d), lambda b, i, j: (b, 0, 0)),
            pl.BlockSpec((1, d), lambda b, i, j: (0, 0)),
            pl.BlockSpec((d, tf), lambda b, i, j: (0, j)),
            pl.BlockSpec((d, tf), lambda b, i, j: (0, j)),
            pl.BlockSpec((tf, d), lambda b, i, j: (j, 0)),
        ],
        out_specs=pl.BlockSpec((1, tm, d), lambda b, i, j: (b, i, 0)),
        out_shape=jax.ShapeDtypeStruct((nb, s, d), F32),
        scratch_shapes=[pltpu.VMEM((tm, d), BF16)],
        compiler_params=_cparams(("arbitrary", "arbitrary", "arbitrary")),
        name="dense_ffn",
    )(x, mod, norm_g.reshape(1, d), w_gate, w_up, w_down)


def _attn_kernel(slopes_ref, q_ref, k_ref, v_ref, dg_ref, lq1_ref, lk1_ref, lq2_ref, lk2_ref,
                 sg_ref, o_ref, vt_ref, ta_ref, tb_ref, m_ref, l_ref, acc_ref, *, lambda_init):
    tq = q_ref.shape[1]
    s_len = k_ref.shape[1]
    head = pl.program_id(1)
    qi = pl.program_id(2)
    slope = jnp.full((1, LANES), slopes_ref[head], F32)

    @pl.when(qi == 0)
    def _():
        for c in range(s_len // tq):
            blk = v_ref[0, c * tq:(c + 1) * tq, :].astype(F32)
            vt_ref[:, c * tq:(c + 1) * tq] = blk.T.astype(BF16)

    lam = (jnp.exp(jnp.sum(lq1_ref[...] * lk1_ref[...], axis=-1, keepdims=True))
           - jnp.exp(jnp.sum(lq2_ref[...] * lk2_ref[...], axis=-1, keepdims=True))
           + lambda_init)

    q = q_ref[0]
    lane = lax.broadcasted_iota(jnp.int32, q.shape, 1)
    q1 = jnp.where(lane < HEAD_DIM, q, jnp.zeros_like(q))
    q2 = jnp.where(lane >= HEAD_DIM, q, jnp.zeros_like(q))
    nt = (((1,), (1,)), ((), ()))
    q0 = qi * tq

    s_hi = slope.astype(BF16).astype(F32)
    s_mid = (slope - s_hi).astype(BF16).astype(F32)
    s_lo = ((slope - s_hi) - s_mid).astype(BF16).astype(F32)
    alane = lax.broadcasted_iota(jnp.int32, (2 * tq, LANES), 1)
    piece = jnp.where(alane % 3 == 0, s_hi, jnp.where(alane % 3 == 1, s_mid, s_lo))
    q_aug = jnp.where(alane < 6, piece, 0.0).astype(BF16)
    qq = jnp.concatenate([jnp.concatenate([q1, q2], axis=0), q_aug], axis=1)
    krow = lax.broadcasted_iota(jnp.int32, (tq, LANES), 0)
    klane = lax.broadcasted_iota(jnp.int32, (tq, LANES), 1)
    j_split = jnp.where(klane < 3, krow % BF16_EXACT_INT, (krow // BF16_EXACT_INT) * BF16_EXACT_INT)
    k_aug = jnp.where(klane < 6, j_split, 0).astype(F32).astype(BF16)

    def scores(c, t_ref):
        ks = pl.multiple_of(c * tq, tq)
        kc = jnp.concatenate([k_ref[0, pl.ds(ks, tq), :], k_aug], axis=1)
        t_ref[...] = lax.dot_general(kc, qq, nt, preferred_element_type=F32)

    def absorb(c, t_ref, bias, shift):
        ks = pl.multiple_of(c * tq, tq)
        t = t_ref[...] if bias is None else t_ref[...] + bias
        m = m_ref[...]
        m_new = jnp.maximum(m, jnp.max(t, axis=0, keepdims=True) + shift)
        p = jnp.exp2(t - (m_new - shift))
        alpha = jnp.exp2(m - m_new)
        l_ref[...] = alpha * l_ref[...] + jnp.sum(p, axis=0, keepdims=True)
        acc_ref[...] = alpha * acc_ref[...] + jnp.dot(vt_ref[:, pl.ds(ks, tq)], p.astype(BF16),
                                                      preferred_element_type=F32)
        m_ref[...] = m_new

    def absorb_past(c, t_ref):
        absorb(c, t_ref, None, slope[:, 0:1] * (c * tq - q0).astype(F32))

    def absorb_diag(t_ref):
        diag = slope[:, 0:1] * dg_ref[...]
        absorb(qi, t_ref, jnp.concatenate([diag, diag], axis=1), jnp.zeros((1, 1), F32))

    m_ref[...] = jnp.full(m_ref.shape, -jnp.inf, F32)
    l_ref[...] = jnp.zeros(l_ref.shape, F32)
    acc_ref[...] = jnp.zeros(acc_ref.shape, F32)
    scores(0, ta_ref)

    def pair(p, carry):
        c = 2 * p
        scores(c + 1, tb_ref)
        absorb_past(c, ta_ref)
        scores(c + 2, ta_ref)
        absorb_past(c + 1, tb_ref)
        return carry

    lax.fori_loop(0, qi // 2, pair, 0)
    odd = qi % 2 == 1

    @pl.when(odd)
    def _():
        scores(qi, tb_ref)
        absorb_past(qi - 1, ta_ref)
        absorb_diag(tb_ref)

    @pl.when(jnp.logical_not(odd))
    def _():
        absorb_diag(ta_ref)

    acc = acc_ref[...]
    l = l_ref[...]
    o = acc[:, :tq] / l[:, :tq] - lam * (acc[:, tq:] / l[:, tq:])
    o = o * lax.rsqrt(jnp.mean(o * o, axis=0, keepdims=True) + RMS_EPS)
    o_ref[0] = ((o.T * sg_ref[...]) * (1.0 - lambda_init)).astype(o_ref.dtype)


def _diag_bias(tq):
    j = np.arange(tq)[:, None]
    r = np.arange(tq)[None, :]
    tile = (np.minimum(j, 2 * r - j) - j).astype(np.float32)
    return jnp.asarray(np.where(j // CHUNK <= r // CHUNK, tile, -np.inf).astype(np.float32))


def _diff_attention(qkv, lam_q1, lam_k1, lam_q2, lam_k2, subln_g, lambda_init):
    nb, s, n3 = qkv.shape
    d = n3 // 3
    heads = d // HEAD_LANES
    tq = _tile(s, ATTN_BLOCK)
    ratio = 2.0 ** (-8.0 / heads)
    slopes = np.array([ratio ** (h + 1) for h in range(heads)], dtype=np.float32)
    slopes2 = jnp.asarray(slopes * np.float32(LOG2E))
    vec = lambda a: a.reshape(1, HEAD_DIM)
    small = pl.BlockSpec((1, HEAD_DIM), lambda b, h, i: (0, 0))
    return pl.pallas_call(
        functools.partial(_attn_kernel, lambda_init=lambda_init),
        grid=(nb, heads, s // tq),
        in_specs=[
            pl.BlockSpec(memory_space=pltpu.SMEM),
            pl.BlockSpec((1, tq, HEAD_LANES), lambda b, h, i: (b, i, h)),
            pl.BlockSpec((1, s, HEAD_LANES), lambda b, h, i: (b, 0, heads + h)),
            pl.BlockSpec((1, s, HEAD_LANES), lambda b, h, i: (b, 0, 2 * heads + h)),
            pl.BlockSpec((tq, tq), lambda b, h, i: (0, 0)),
            small, small, small, small,
            pl.BlockSpec((1, HEAD_LANES), lambda b, h, i: (0, 0)),
        ],
        out_specs=pl.BlockSpec((1, tq, HEAD_LANES), lambda b, h, i: (b, i, h)),
        out_shape=jax.ShapeDtypeStruct((nb, s, d), BF16),
        scratch_shapes=[pltpu.VMEM((HEAD_LANES, s), BF16),
                        pltpu.VMEM((tq, 2 * tq), F32), pltpu.VMEM((tq, 2 * tq), F32),
                        pltpu.VMEM((1, 2 * tq), F32), pltpu.VMEM((1, 2 * tq), F32),
                        pltpu.VMEM((HEAD_LANES, 2 * tq), F32)],
        compiler_params=_cparams(("arbitrary", "arbitrary", "arbitrary")),
        name="diff_attention",
    )(slopes2, qkv, qkv, qkv, _diag_bias(tq), vec(lam_q1), vec(lam_k1), vec(lam_q2), vec(lam_k2),
      subln_g.reshape(1, HEAD_LANES))


R_E1, R_E2, R_W1, R_W2, R_RANK1, R_RANK2 = range(6)


def _wo_router_kernel(a_ref, x_ref, mod_ref, g_ref, wo_ref, wr_ref, x_out, h_out, r_out, cnt_out,
                      cnt_ref):
    tm = a_ref.shape[1]

    @pl.when((pl.program_id(0) == 0) & (pl.program_id(1) == 0))
    def _():
        cnt_ref[...] = jnp.zeros_like(cnt_ref)

    mix = jnp.dot(a_ref[0], wo_ref[...], preferred_element_type=F32)
    xn = x_ref[0] + mod_ref[0, 2:3, :] * mix
    x_out[0] = xn
    inv = lax.rsqrt(jnp.mean(xn * xn, axis=-1, keepdims=True) + RMS_EPS)
    h = ((xn * inv) * g_ref[...]) * (1.0 + mod_ref[0, 4:5, :]) + mod_ref[0, 3:4, :]
    h_out[0] = h

    lane = lax.broadcasted_iota(jnp.int32, (tm, LANES), 1).astype(F32)
    logits = jnp.full((tm, LANES), -jnp.inf, F32)
    for e in range(N_EXPERTS):
        col = jnp.sum(h * wr_ref[e:e + 1, :], axis=-1, keepdims=True)
        logits = jnp.where(lane == e, col, logits)
    v1 = jnp.max(logits, axis=-1, keepdims=True)
    e1 = jnp.min(jnp.where(logits == v1, lane, float(LANES)), axis=-1, keepdims=True)
    hot1 = lane == e1
    rest = jnp.where(hot1, -jnp.inf, logits)
    v2 = jnp.max(rest, axis=-1, keepdims=True)
    e2 = jnp.min(jnp.where(rest == v2, lane, float(LANES)), axis=-1, keepdims=True)
    hot2 = lane == e2
    ex = jnp.exp(v2 - v1)
    w1 = 1.0 / (1.0 + ex)
    w2 = ex / (1.0 + ex)

    hot = jnp.where(hot1 | hot2, 1.0, 0.0)
    ri = lax.broadcasted_iota(jnp.int32, (tm, tm), 0)
    ci = lax.broadcasted_iota(jnp.int32, (tm, tm), 1)
    tri = jnp.where(ci < ri, 1.0, 0.0).astype(BF16)
    before = jnp.dot(tri, hot.astype(BF16), preferred_element_type=F32) + cnt_ref[...]
    rank1 = jnp.sum(jnp.where(hot1, before, 0.0), axis=-1, keepdims=True)
    rank2 = jnp.sum(jnp.where(hot2, before, 0.0), axis=-1, keepdims=True)
    cnt_ref[...] += jnp.sum(hot, axis=0, keepdims=True)
    cnt_out[...] = cnt_ref[...]

    rec = jnp.where(lane == R_E1, e1, 0.0)
    rec = jnp.where(lane == R_E2, e2, rec)
    rec = jnp.where(lane == R_W1, w1, rec)
    rec = jnp.where(lane == R_W2, w2, rec)
    rec = jnp.where(lane == R_RANK1, rank1, rec)
    rec = jnp.where(lane == R_RANK2, rank2, rec)
    r_out[0] = rec


def _wo_router(attn, x, mod, norm_g, w_o, w_router):
    nb, s, d = x.shape
    tm = _tile(s, 512)
    wr = w_router.T
    return pl.pallas_call(
        _wo_router_kernel,
        grid=(nb, s // tm),
        in_specs=[
            pl.BlockSpec((1, tm, d), lambda b, i: (b, i, 0)),
            pl.BlockSpec((1, tm, d), lambda b, i: (b, i, 0)),
            pl.BlockSpec((1, 6, d), lambda b, i: (b, 0, 0)),
            pl.BlockSpec((1, d), lambda b, i: (0, 0)),
            pl.BlockSpec((d, d), lambda b, i: (0, 0)),
            pl.BlockSpec((N_EXPERTS, d), lambda b, i: (0, 0)),
        ],
        out_specs=[
            pl.BlockSpec((1, tm, d), lambda b, i: (b, i, 0)),
            pl.BlockSpec((1, tm, d), lambda b, i: (b, i, 0)),
            pl.BlockSpec((1, tm, LANES), lambda b, i: (b, i, 0)),
            pl.BlockSpec((1, LANES), lambda b, i: (0, 0)),
        ],
        out_shape=[
            jax.ShapeDtypeStruct((nb, s, d), F32),
            jax.ShapeDtypeStruct((nb, s, d), F32),
            jax.ShapeDtypeStruct((nb, s, LANES), F32),
            jax.ShapeDtypeStruct((1, LANES), F32),
        ],
        scratch_shapes=[pltpu.VMEM((1, LANES), F32)],
        compiler_params=_cparams(("arbitrary", "arbitrary")),
        name="wo_router",
    )(attn, x, mod, norm_g.reshape(1, d), w_o, wr)


def _dispatch_kernel(pos_ref, h_ref, xs_in, xs_out, sem, *, td):
    del xs_in

    def row_copy(t, p):
        return pltpu.make_async_copy(h_ref.at[pl.ds(t, 1)], xs_out.at[pl.ds(p, 1)], sem)

    def issue(t, carry):
        row_copy(t, pos_ref[2 * t]).start(priority=0)
        row_copy(t, pos_ref[2 * t + 1]).start(priority=1)
        return carry

    lax.fori_loop(0, td, issue, 0, unroll=DMA_UNROLL)

    def drain(t, carry):
        row_copy(0, 0).wait()
        row_copy(0, 0).wait()
        return carry

    lax.fori_loop(0, td, drain, 0, unroll=DMA_UNROLL)


def _dispatch(h2, pos, xs_zero):
    n, d = h2.shape
    td = _tile(n, 512)
    return pl.pallas_call(
        functools.partial(_dispatch_kernel, td=td),
        grid=(n // td,),
        in_specs=[
            pl.BlockSpec((2 * td,), lambda i: (i,), memory_space=pltpu.SMEM),
            pl.BlockSpec((td, d), lambda i: (i, 0)),
            pl.BlockSpec(memory_space=pl.ANY),
        ],
        out_specs=pl.BlockSpec(memory_space=pl.ANY),
        out_shape=jax.ShapeDtypeStruct(xs_zero.shape, xs_zero.dtype),
        scratch_shapes=[pltpu.SemaphoreType.DMA(())],
        input_output_aliases={2: 0},
        compiler_params=_cparams(("arbitrary",)),
        name="dispatch",
    )(pos.reshape(-1), h2, xs_zero)


def _moe_kernel(be_ref, bv_ref, xs_ref, wg_ref, wu_ref, wd_ref, o_ref, xb_ref):
    i = pl.program_id(0)
    j = pl.program_id(1)
    valid = bv_ref[i] > 0

    @pl.when(valid & (j == 0))
    def _():
        xb_ref[...] = xs_ref[...].astype(BF16)

    @pl.when(valid)
    def _():
        xb = xb_ref[...]
        gate = jnp.dot(xb, wg_ref[0], preferred_element_type=F32)
        up = jnp.dot(xb, wu_ref[0], preferred_element_type=F32)
        a = ((gate * _sigmoid(gate)) * up).astype(BF16)
        part = jnp.dot(a, wd_ref[0], preferred_element_type=F32)

        @pl.when(j == 0)
        def _():
            o_ref[...] = part

        @pl.when(j > 0)
        def _():
            o_ref[...] += part

    @pl.when(jnp.logical_not(valid) & (j == 0))
    def _():
        o_ref[...] = jnp.zeros_like(o_ref)


def _moe_experts(xs, blk_expert, blk_valid, w_gate, w_up, w_down, tme):
    p, d = xs.shape
    f = w_gate.shape[2]
    tf = _tile(f, 1024)
    nf = f // tf

    def fj(i, j, bv):
        return jnp.where(bv[i] > 0, j, nf - 1)

    grid_spec = pltpu.PrefetchScalarGridSpec(
        num_scalar_prefetch=2,
        grid=(p // tme, nf),
        in_specs=[
            pl.BlockSpec((tme, d), lambda i, j, be, bv: (i, 0)),
            pl.BlockSpec((1, d, tf), lambda i, j, be, bv: (be[i], 0, fj(i, j, bv))),
            pl.BlockSpec((1, d, tf), lambda i, j, be, bv: (be[i], 0, fj(i, j, bv))),
            pl.BlockSpec((1, tf, d), lambda i, j, be, bv: (be[i], fj(i, j, bv), 0)),
        ],
        out_specs=pl.BlockSpec((tme, d), lambda i, j, be, bv: (i, 0)),
        scratch_shapes=[pltpu.VMEM((tme, d), BF16)],
    )
    return pl.pallas_call(
        _moe_kernel,
        grid_spec=grid_spec,
        out_shape=jax.ShapeDtypeStruct((p, d), F32),
        compiler_params=_cparams(("arbitrary", "arbitrary")),
        name="moe_experts",
    )(blk_expert, blk_valid, xs, w_gate, w_up, w_down)


def _final_kernel(pos_ref, ys_hbm, x_ref, r_ref, mod_ref, g_ref, o_ref, buf_ref, sem):
    tm = x_ref.shape[1]

    def row_copy(t, k, p):
        return pltpu.make_async_copy(ys_hbm.at[pl.ds(p, 1)], buf_ref.at[k, pl.ds(t, 1)], sem)

    def issue(t, carry):
        row_copy(t, 0, pos_ref[2 * t]).start(priority=0)
        row_copy(t, 1, pos_ref[2 * t + 1]).start(priority=1)
        return carry

    lax.fori_loop(0, tm, issue, 0, unroll=DMA_UNROLL)

    def drain(t, carry):
        row_copy(0, 0, 0).wait()
        row_copy(0, 1, 0).wait()
        return carry

    lax.fori_loop(0, tm, drain, 0, unroll=DMA_UNROLL)

    rec = r_ref[0]
    w1 = rec[:, R_W1:R_W1 + 1]
    w2 = rec[:, R_W2:R_W2 + 1]
    y = w1 * buf_ref[0] + w2 * buf_ref[1]
    xn = x_ref[0] + mod_ref[0, 5:6, :] * y
    inv = lax.rsqrt(jnp.mean(xn * xn, axis=-1, keepdims=True) + RMS_EPS)
    o_ref[0] = (xn * inv) * g_ref[...]


def _combine_final(ys, pos, x, route, mod, final_g):
    nb, s, d = x.shape
    tm = _tile(s, 512)
    ni = s // tm
    return pl.pallas_call(
        _final_kernel,
        grid=(nb, ni),
        in_specs=[
            pl.BlockSpec((2 * tm,), lambda b, i: (b * ni + i,), memory_space=pltpu.SMEM),
            pl.BlockSpec(memory_space=pl.ANY),
            pl.BlockSpec((1, tm, d), lambda b, i: (b, i, 0)),
            pl.BlockSpec((1, tm, LANES), lambda b, i: (b, i, 0)),
            pl.BlockSpec((1, 6, d), lambda b, i: (b, 0, 0)),
            pl.BlockSpec((1, d), lambda b, i: (0, 0)),
        ],
        out_specs=pl.BlockSpec((1, tm, d), lambda b, i: (b, i, 0)),
        out_shape=jax.ShapeDtypeStruct((nb, s, d), F32),
        scratch_shapes=[pltpu.VMEM((2, tm, d), F32), pltpu.SemaphoreType.DMA(())],
        compiler_params=_cparams(("arbitrary", "arbitrary")),
        name="combine_final",
    )(pos.reshape(-1), ys, x, route, mod, final_g.reshape(1, d))


def _lambda_init(layer_idx):
    return 0.8 - 0.6 * math.exp(-0.3 * layer_idx)


def kernel(x, c, mod_w, mod_b, norm1_g, norm2_g, conv_w_in, conv_b_in, conv_w_dw, conv_b_dw, conv_ln_g, conv_ln_b, conv_w_out, conv_b_out, attn_w_qkv, attn_w_o, attn_lam_q1, attn_lam_k1, attn_lam_q2, attn_lam_k2, attn_subln_g, ffn_w_gate, ffn_w_up, ffn_w_down, moe_w_router, moe_w_gate, moe_w_up, moe_w_down, final_g):
    nb, s, d = x.shape
    n = nb * s
    bf = lambda w: w.astype(BF16)

    mod = _modulation(c, mod_w, mod_b).reshape(mod_w.shape[0], nb, 6, d)

    u = _conv_in(x, mod[0], norm1_g[0], bf(conv_w_in[0]), conv_b_in[0])
    x = _conv_out(u, x, mod[0], conv_w_dw[0], conv_b_dw[0], conv_ln_g[0], conv_ln_b[0],
                  bf(conv_w_out[0]), conv_b_out[0])
    x = _dense_ffn(x, mod[0], norm2_g[0], bf(ffn_w_gate[0]), bf(ffn_w_up[0]), bf(ffn_w_down[0]))

    qkv = _qkv_proj(x, mod[1], norm1_g[1], bf(attn_w_qkv[0]))
    attn = _diff_attention(qkv, attn_lam_q1[0], attn_lam_k1[0], attn_lam_q2[0], attn_lam_k2[0],
                           attn_subln_g[0], _lambda_init(1))
    x, h2, route, counts = _wo_router(attn, x, mod[1], norm2_g[1], bf(attn_w_o[0]),
                                      moe_w_router[0])

    tme = _tile(n, 512)
    nblk = 2 * n // tme + N_EXPERTS
    route2 = route.reshape(n, LANES)
    eidx = route2[:, R_E1:R_E2 + 1].astype(jnp.int32)
    rank = route2[:, R_RANK1:R_RANK2 + 1].astype(jnp.int32)
    cnt = counts[0, :N_EXPERTS].astype(jnp.int32)
    blocks = (cnt + tme - 1) // tme
    blk_end = jnp.cumsum(blocks)
    start = (blk_end - blocks) * tme
    pos = start[eidx] + rank
    total = blk_end[-1]
    bid = jnp.arange(nblk, dtype=jnp.int32)
    blk_valid = (bid < total).astype(jnp.int32)
    blk_expert = jnp.searchsorted(blk_end, jnp.minimum(bid, total - 1), side="right")
    blk_expert = jnp.minimum(blk_expert, N_EXPERTS - 1).astype(jnp.int32)

    xs = _dispatch(h2.reshape(n, d), pos, jnp.zeros((nblk * tme, d), F32))
    ys = _moe_experts(xs, blk_expert, blk_valid, bf(moe_w_gate[0]), bf(moe_w_up[0]),
                      bf(moe_w_down[0]), tme)
    return _combine_final(ys, pos, x, route, mod[1], final_g)
```
